```python
import jax, jax.numpy as jnp
from jax import lax
import numpy as np

D_MODEL = 2048
BATCH = 8
SEQ = 4096
DEPTH = 1

N_META = 16
BLK = 128
PAD_FRONT = BLK - N_META
MLA_HEADS = 8
MLA_NOPE = 128
MLA_ROPE = 64
MLA_V = 128
MLA_Q_RANK = 512
MLA_KV_RANK = 256
MLA_WIDTH = MLA_HEADS * MLA_V
RET_HEADS = 8
RET_DK = 128
RET_DV = 128
RET_WIDTH = RET_HEADS * RET_DV
ROPE_BASE = 10000.0
NORM_EPS = 1e-6
GN_EPS = 1e-5
N_BRANCH = 2
NEG_INF = -1e30
IN_SPLITS = (MLA_Q_RANK, MLA_KV_RANK, MLA_ROPE, MLA_WIDTH,
             RET_HEADS * RET_DK, RET_HEADS * RET_DK, RET_WIDTH, RET_WIDTH,
             N_BRANCH * D_MODEL)
IN_WIDTH = sum(IN_SPLITS)

kernel_name = "mla_retention_gated_hybrid"


def rmsnorm(x, w):
    xf = x.astype(jnp.float32)
    y = xf * lax.rsqrt(jnp.mean(xf * xf, axis=-1, keepdims=True) + NORM_EPS)
    return (y * w.astype(jnp.float32)).astype(x.dtype)


def rope(t, pos):
    d = t.shape[-1]
    inv = ROPE_BASE ** (-jnp.arange(0, d, 2, dtype=jnp.float32) / d)
    ang = pos.astype(jnp.float32)[:, None] * inv[None, :]
    ang = ang.reshape(ang.shape[:1] + (1,) * (t.ndim - 3) + ang.shape[1:])
    cos = jnp.cos(ang).astype(t.dtype)
    sin = jnp.sin(ang).astype(t.dtype)
    t1, t2 = t[..., : d // 2], t[..., d // 2:]
    return jnp.concatenate([t1 * cos - t2 * sin, t1 * sin + t2 * cos], axis=-1)


def pad_front(t):
    return jnp.pad(t, ((0, 0), (PAD_FRONT, 0)) + ((0, 0),) * (t.ndim - 2))


def mla_branch(c_q, c_kv, k_pe, pos, q_norm_w, w_uq, kv_norm_w, w_ukv):
    B, L, _ = c_q.shape
    q = (rmsnorm(c_q, q_norm_w) @ w_uq).reshape(B, L, MLA_HEADS, MLA_NOPE + MLA_ROPE)
    q_nope, q_pe = q[..., :MLA_NOPE], rope(q[..., MLA_NOPE:], pos)
    kv = (rmsnorm(c_kv, kv_norm_w) @ w_ukv).reshape(B, L, MLA_HEADS, MLA_NOPE + MLA_V)
    k_nope, v = kv[..., :MLA_NOPE], kv[..., MLA_NOPE:]
    k_pe = rope(k_pe, pos)
    q_nope, q_pe, k_nope, v, k_pe = (pad_front(t) for t in (q_nope, q_pe, k_nope, v, k_pe))
    Lp = L + PAD_FRONT
    scale = (MLA_NOPE + MLA_ROPE) ** -0.5
    kidx = jnp.arange(Lp)

    def block(i):
        start = i * BLK
        qn = lax.dynamic_slice_in_dim(q_nope, start, BLK, axis=1)
        qp = lax.dynamic_slice_in_dim(q_pe, start, BLK, axis=1)
        s = (jnp.einsum('bqhd,bkhd->bhqk', qn, k_nope)
             + jnp.einsum('bqhd,bkd->bhqk', qp, k_pe)).astype(jnp.float32) * scale
        qidx = start + jnp.arange(BLK)
        mask = (kidx[None, :] <= qidx[:, None]) & (kidx[None, :] >= PAD_FRONT)
        s = jnp.where(mask[None, None], s, NEG_INF)
        p = jax.nn.softmax(s, axis=-1).astype(v.dtype)
        return jnp.einsum('bhqk,bkhd->bqhd', p, v)

    o = lax.map(block, jnp.arange(Lp // BLK))
    o = o.transpose(1, 0, 2, 3, 4).reshape(B, Lp, MLA_WIDTH)
    return o[:, PAD_FRONT:]


def retention_branch(r_q, r_k, r_v, pos, gn_w, gn_b):
    B, L, _ = r_q.shape
    dt = r_q.dtype
    q = rope(r_q.reshape(B, L, RET_HEADS, RET_DK), pos)
    k = rope(r_k.reshape(B, L, RET_HEADS, RET_DK), pos) * (RET_DK ** -0.5)
    v = r_v.reshape(B, L, RET_HEADS, RET_DV)
    q, k, v = (pad_front(t).astype(jnp.float32) for t in (q, k, v))
    Lp = L + PAD_FRONT
    nc = Lp // BLK

    def chunk(t):
        return t.reshape(B, nc, BLK, RET_HEADS, -1).transpose(0, 3, 1, 2, 4)

    qc, kc, vc = chunk(q), chunk(k), chunk(v)
    log_g = jnp.log1p(-(2.0 ** (-5.0 - jnp.arange(RET_HEADS, dtype=jnp.float32))))
    n = jnp.arange(BLK, dtype=jnp.float32)
    diff = n[:, None] - n[None, :]
    decay_in = jnp.where(diff >= 0, jnp.exp(log_g[:, None, None] * jnp.maximum(diff, 0.0)), 0.0)
    zeta = jnp.exp(log_g[:, None] * (BLK - 1.0 - n))
    xi = jnp.exp(log_g[:, None] * (n + 1.0))
    g_chunk = jnp.exp(log_g * BLK)
    s = jnp.einsum('bhcnd,bhcmd->bhcnm', qc, kc) * decay_in[None, :, None]
    inner = jnp.einsum('bhcnm,bhcme->bhcne', s, vc)
    kv_chunk = jnp.einsum('bhcmd,bhcme->cbhde', kc * zeta[None, :, None, :, None], vc)

    def step(R, kv):
        return R * g_chunk[None, :, None, None] + kv, R

    _, R_prev = lax.scan(step, jnp.zeros((B, RET_HEADS, RET_DK, RET_DV), jnp.float32), kv_chunk)
    cross = jnp.einsum('bhcnd,cbhde->bhcne', qc, R_prev) * xi[None, :, None, :, None]
    o = (inner + cross).transpose(0, 2, 3, 1, 4).reshape(B, Lp, RET_HEADS, RET_DV)[:, PAD_FRONT:]
    mu = jnp.mean(o, axis=-1, keepdims=True)
    var = jnp.mean(jnp.square(o - mu), axis=-1, keepdims=True)
    o = ((o - mu) * lax.rsqrt(var + GN_EPS)).reshape(B, L, RET_WIDTH)
    o = o * gn_w.astype(jnp.float32) + gn_b.astype(jnp.float32)
    return o.astype(dt)


def hybrid_layer(h, pos, norm_w, w_in, mla_q_norm_w, mla_w_uq, mla_kv_norm_w, mla_w_ukv,
                 ret_gn_w, ret_gn_b, w_branch_mla, w_branch_ret, w_out):
    B, L, D = h.shape
    xn = rmsnorm(h, norm_w)
    proj = xn @ w_in
    offs, acc = [], 0
    for w in IN_SPLITS[:-1]:
        acc += w
        offs.append(acc)
    c_q, c_kv, k_pe, z_mla, r_q, r_k, r_v, z_ret, gate_logits = jnp.split(proj, offs, axis=-1)
    y_mla = mla_branch(c_q, c_kv, k_pe, pos, mla_q_norm_w, mla_w_uq, mla_kv_norm_w, mla_w_ukv) * jax.nn.silu(z_mla)
    y_ret = retention_branch(r_q, r_k, r_v, pos, ret_gn_w, ret_gn_b) * jax.nn.silu(z_ret)
    gates = jax.nn.sigmoid(gate_logits.astype(jnp.float32)).astype(h.dtype).reshape(B, L, N_BRANCH, D)
    merged = gates[:, :, 0] * (y_mla @ w_branch_mla) + gates[:, :, 1] * (y_ret @ w_branch_ret)
    return h + merged @ w_out


def setup_inputs(seed: int = 0) -> dict:
    key = jax.random.key(seed)
    ks = jax.random.split(key, 16)
    f32 = jnp.float32

    def w(k, shape, fan_in):
        return jax.random.normal(k, shape, f32) * (fan_in ** -0.5)

    def gain(k, shape):
        return 1.0 + 0.02 * jax.random.normal(k, shape, f32)

    return {
        "x": jax.random.normal(ks[0], (BATCH, SEQ, D_MODEL), f32),
        "meta": jax.random.normal(ks[1], (N_META, D_MODEL), f32),
        "norm_w": gain(ks[2], (DEPTH, D_MODEL)),
        "w_in": w(ks[3], (DEPTH, D_MODEL, IN_WIDTH), D_MODEL),
        "mla_q_norm_w": gain(ks[4], (DEPTH, MLA_Q_RANK)),
        "mla_w_uq": w(ks[5], (DEPTH, MLA_Q_RANK, MLA_HEADS * (MLA_NOPE + MLA_ROPE)), MLA_Q_RANK),
        "mla_kv_norm_w": gain(ks[6], (DEPTH, MLA_KV_RANK)),
        "mla_w_ukv": w(ks[7], (DEPTH, MLA_KV_RANK, MLA_HEADS * (MLA_NOPE + MLA_V)), MLA_KV_RANK),
        "ret_gn_w": gain(ks[8], (DEPTH, RET_WIDTH)),
        "ret_gn_b": 0.02 * jax.random.normal(ks[9], (DEPTH, RET_WIDTH), f32),
        "w_branch_mla": w(ks[10], (DEPTH, MLA_WIDTH, D_MODEL), MLA_WIDTH),
        "w_branch_ret": w(ks[11], (DEPTH, RET_WIDTH, D_MODEL), RET_WIDTH),
        "w_out": w(ks[12], (DEPTH, D_MODEL, D_MODEL), D_MODEL),
        "final_norm_w": gain(ks[13], (D_MODEL,)),
    }


def reference(x, meta, norm_w, w_in, mla_q_norm_w, mla_w_uq, mla_kv_norm_w, mla_w_ukv,
              ret_gn_w, ret_gn_b, w_branch_mla, w_branch_ret, w_out, final_norm_w):
    B = x.shape[0]
    h = jnp.concatenate([jnp.broadcast_to(meta.astype(x.dtype)[None], (B, N_META, D_MODEL)), x], axis=1)
    pos = jnp.arange(h.shape[1])
    for l in range(DEPTH):
        h = hybrid_layer(h, pos, norm_w[l], w_in[l], mla_q_norm_w[l], mla_w_uq[l],
                         mla_kv_norm_w[l], mla_w_ukv[l], ret_gn_w[l], ret_gn_b[l],
                         w_branch_mla[l], w_branch_ret[l], w_out[l])
    h = rmsnorm(h, final_norm_w)
    return h[:, N_META:]
```

```python
import functools

import numpy as np
import jax
import jax.numpy as jnp
from jax import lax
from jax.experimental import pallas as pl
from jax.experimental.pallas import tpu as pltpu

D_MODEL = 2048
N_META = 16
CHUNK = 128
META_PAD = CHUNK - N_META
HEADS = 8
NOPE = 128
ROPE = 64
HEAD_V = 128
Q_RANK = 512
KV_RANK = 256
QK_PAD = 256
RET_D = 128
ROPE_BASE = 10000.0
NORM_EPS = 1e-6
GN_EPS = 1e-5
NEG_INF = -1e30

GATE_OFF = 0
ZM_OFF = 4096
ZR_OFF = 5120
RQ_OFF = 6144
RK_OFF = 7168
RV_OFF = 8192
CQ_OFF = 9216
CKV_OFF = 9728
KPE_OFF = 9984
IN_PAD = 10240

VMEM_LIMIT = 56 * 1024 * 1024

_BF = jnp.bfloat16
_F32 = jnp.float32


def _dot(a, b):
    return jnp.dot(a, b, preferred_element_type=_F32)


def _dot_nt(a, b):
    return lax.dot_general(a, b, (((1,), (1,)), ((), ())), preferred_element_type=_F32)


def _dot_tn(a, b):
    return lax.dot_general(a, b, (((0,), (0,)), ((), ())), preferred_element_type=_F32)


def _in_proj_kernel(x_ref, nw_ref, w_ref, o_ref, xn_ref, *, tn):
    j = pl.program_id(1)

    @pl.when(j == 0)
    def _():
        x = x_ref[...]
        ms = jnp.mean(x * x, axis=-1, keepdims=True)
        xn_ref[...] = (x * lax.rsqrt(ms + NORM_EPS) * nw_ref[...]).astype(_BF)

    acc = _dot(xn_ref[...], w_ref[...])
    col = j * tn

    @pl.when(col < ZM_OFF)
    def _():
        o_ref[...] = jax.nn.sigmoid(acc).astype(_BF)

    @pl.when((col >= ZM_OFF) & (col < RQ_OFF))
    def _():
        o_ref[...] = (acc * jax.nn.sigmoid(acc)).astype(_BF)

    @pl.when(col >= RQ_OFF)
    def _():
        o_ref[...] = acc.astype(_BF)


def _in_proj(x2d, norm_w, w_in_bf, *, tm, tn):
    rows = x2d.shape[0]
    return pl.pallas_call(
        functools.partial(_in_proj_kernel, tn=tn),
        grid=(rows // tm, IN_PAD // tn),
        in_specs=[
            pl.BlockSpec((tm, D_MODEL), lambda i, j: (i, 0)),
            pl.BlockSpec((1, D_MODEL), lambda i, j: (0, 0)),
            pl.BlockSpec((D_MODEL, tn), lambda i, j: (0, j)),
        ],
        out_specs=pl.BlockSpec((tm, tn), lambda i, j: (i, j)),
        out_shape=jax.ShapeDtypeStruct((rows, IN_PAD), _BF),
        scratch_shapes=[pltpu.VMEM((tm, D_MODEL), _BF)],
        compiler_params=pltpu.CompilerParams(
            dimension_semantics=("parallel", "arbitrary"), vmem_limit_bytes=VMEM_LIMIT),
        name="in_proj",
    )(x2d, norm_w, w_in_bf)


def _rms(x, w):
    ms = jnp.mean(x * x, axis=-1, keepdims=True)
    return x * lax.rsqrt(ms + NORM_EPS) * w


def _rope64(t, c, s1, s2):
    return t * c + pltpu.roll(t, 96, 1) * s1 + pltpu.roll(t, 32, 1) * s2


def _mla_prep_kernel(cq_ref, ckv_ref, kpe_ref, qnw_ref, kvnw_ref, wq_ref, wk_ref, wv_ref,
                     c_ref, s1_ref, s2_ref, q_ref, k_ref, v_ref, *, scale):
    c, s1, s2 = c_ref[...], s1_ref[...], s2_ref[...]

    cqn = _rms(cq_ref[...].astype(_F32), qnw_ref[...]).astype(_BF)
    q = _dot(cqn, wq_ref[...])
    for h in range(HEADS):
        lo = h * QK_PAD
        q_ref[:, lo:lo + NOPE] = (q[:, lo:lo + NOPE] * scale).astype(_BF)
        q_ref[:, lo + NOPE:lo + QK_PAD] = (
            _rope64(q[:, lo + NOPE:lo + QK_PAD], c, s1, s2) * scale).astype(_BF)

    ckvn = _rms(ckv_ref[...].astype(_F32), kvnw_ref[...]).astype(_BF)
    kn = _dot(ckvn, wk_ref[...])
    kpe = _rope64(kpe_ref[...].astype(_F32), c, s1, s2).astype(_BF)
    for h in range(HEADS):
        lo = h * QK_PAD
        k_ref[:, lo:lo + NOPE] = kn[:, h * NOPE:(h + 1) * NOPE].astype(_BF)
        k_ref[:, lo + NOPE:lo + QK_PAD] = kpe
    v_ref[...] = _dot(ckvn, wv_ref[...]).astype(_BF)


def _mla_prep(proj, qnw, kvnw, wq, wk, wv, tabs, *, tm, scale):
    rows = proj.shape[0]
    tab_blocks = tabs[0].shape[0] // tm
    full = lambda shape: pl.BlockSpec(shape, lambda i: (0, 0))
    tab_spec = pl.BlockSpec((tm, 128), lambda i: (i % tab_blocks, 0))
    return pl.pallas_call(
        functools.partial(_mla_prep_kernel, scale=scale),
        grid=(rows // tm,),
        in_specs=[
            pl.BlockSpec((tm, Q_RANK), lambda i: (i, CQ_OFF // Q_RANK)),
            pl.BlockSpec((tm, KV_RANK), lambda i: (i, CKV_OFF // KV_RANK)),
            pl.BlockSpec((tm, 128), lambda i: (i, KPE_OFF // 128)),
            full((1, Q_RANK)), full((1, KV_RANK)),
            full((Q_RANK, HEADS * QK_PAD)), full((KV_RANK, HEADS * NOPE)),
            full((KV_RANK, HEADS * HEAD_V)),
            tab_spec, tab_spec, tab_spec,
        ],
        out_specs=[
            pl.BlockSpec((tm, HEADS * QK_PAD), lambda i: (i, 0)),
            pl.BlockSpec((tm, HEADS * QK_PAD), lambda i: (i, 0)),
            pl.BlockSpec((tm, HEADS * HEAD_V), lambda i: (i, 0)),
        ],
        out_shape=[
            jax.ShapeDtypeStruct((rows, HEADS * QK_PAD), _BF),
            jax.ShapeDtypeStruct((rows, HEADS * QK_PAD), _BF),
            jax.ShapeDtypeStruct((rows, HEADS * HEAD_V), _BF),
        ],
        compiler_params=pltpu.CompilerParams(
            dimension_semantics=("parallel",), vmem_limit_bytes=VMEM_LIMIT),
        name="mla_prep",
    )(proj, proj, proj, qnw, kvnw, wq, wk, wv, *tabs)


def _attn_kernel(q_ref, k_ref, v_ref, km_ref, vm_ref, zs_ref, o_ref, m_ref, l_ref, acc_ref,
                 *, seq, tq):
    def update(q, k, v, mask):
        s = _dot_nt(q, k)
        if mask is not None:
            s = jnp.where(mask, s, NEG_INF)
        m_old = m_ref[...]
        m_new = jnp.maximum(m_old, jnp.max(s, axis=-1, keepdims=True))
        alpha = jnp.exp(m_old - m_new)
        p = jnp.exp(s - m_new)
        l_ref[...] = alpha * l_ref[...] + jnp.sum(p, axis=-1, keepdims=True)
        acc_ref[...] = alpha * acc_ref[...] + _dot(p.astype(_BF), v)
        m_ref[...] = m_new

    meta_mask = lax.broadcasted_iota(jnp.int32, (tq, CHUNK), 1) >= META_PAD
    row = lax.broadcasted_iota(jnp.int32, (tq, tq), 0)
    col = lax.broadcasted_iota(jnp.int32, (tq, tq), 1)
    causal = col <= row

    for qi in range(seq // tq):
        q = q_ref[0, qi * tq:(qi + 1) * tq, :]
        m_ref[...] = jnp.full((tq, 1), NEG_INF, _F32)
        l_ref[...] = jnp.zeros((tq, 1), _F32)
        acc_ref[...] = jnp.zeros((tq, HEAD_V), _F32)
        update(q, km_ref[...], vm_ref[...], meta_mask)

        def body(ki, carry):
            ks = pl.multiple_of(ki * tq, tq)
            update(q, k_ref[0, pl.ds(ks, tq), :], v_ref[0, pl.ds(ks, tq), :], None)
            return carry

        if qi > 0:
            lax.fori_loop(0, qi, body, 0)
        update(q, k_ref[0, qi * tq:(qi + 1) * tq, :], v_ref[0, qi * tq:(qi + 1) * tq, :], causal)

        y = acc_ref[...] * (1.0 / l_ref[...])
        o_ref[0, qi * tq:(qi + 1) * tq, :] = (
            y * zs_ref[0, qi * tq:(qi + 1) * tq, :].astype(_F32)).astype(_BF)


def _attention(q, k, v, k_meta, v_meta, proj3, *, tq):
    batch, seq = q.shape[0], q.shape[1]
    return pl.pallas_call(
        functools.partial(_attn_kernel, seq=seq, tq=tq),
        grid=(batch, HEADS),
        in_specs=[
            pl.BlockSpec((1, seq, QK_PAD), lambda b, h: (b, 0, h)),
            pl.BlockSpec((1, seq, QK_PAD), lambda b, h: (b, 0, h)),
            pl.BlockSpec((1, seq, HEAD_V), lambda b, h: (b, 0, h)),
            pl.BlockSpec((CHUNK, QK_PAD), lambda b, h: (0, h)),
            pl.BlockSpec((CHUNK, HEAD_V), lambda b, h: (0, h)),
            pl.BlockSpec((1, seq, HEAD_V), lambda b, h: (b, 0, ZM_OFF // HEAD_V + h)),
        ],
        out_specs=pl.BlockSpec((1, seq, HEAD_V), lambda b, h: (b, 0, h)),
        out_shape=jax.ShapeDtypeStruct((batch, seq, HEADS * HEAD_V), _BF),
        scratch_shapes=[pltpu.VMEM((tq, 1), _F32), pltpu.VMEM((tq, 1), _F32),
                        pltpu.VMEM((tq, HEAD_V), _F32)],
        compiler_params=pltpu.CompilerParams(
            dimension_semantics=("parallel", "parallel"), vmem_limit_bytes=VMEM_LIMIT),
        name="attention",
    )(q, k, v, k_meta, v_meta, proj3)


def _rope128(t, c, s):
    return t * c + pltpu.roll(t, 64, 1) * s


def _ret_kernel(rq_ref, rk_ref, rv_ref, zs_ref, rkm_ref, rvm_ref, c_ref, s_ref, cm_ref, sm_ref,
                dec_ref, zeta_ref, xi_ref, gch_ref, gw_ref, gb_ref, o_ref, *, seq):
    kscale = RET_D ** -0.5
    dec = dec_ref[0]
    zeta = zeta_ref[0]
    xi = xi_ref[0]
    gch = gch_ref[0]
    gw = gw_ref[...]
    gb = gb_ref[...]

    km = _rope128(rkm_ref[...].astype(_F32), cm_ref[...], sm_ref[...]) * kscale
    state0 = _dot_tn((km * zeta).astype(_BF), rvm_ref[...])

    def body(ci, state):
        rs = pl.multiple_of(ci * CHUNK, CHUNK)
        c = c_ref[pl.ds(rs, CHUNK), :]
        s = s_ref[pl.ds(rs, CHUNK), :]
        q = _rope128(rq_ref[0, pl.ds(rs, CHUNK), :].astype(_F32), c, s)
        k = _rope128(rk_ref[0, pl.ds(rs, CHUNK), :].astype(_F32), c, s) * kscale
        v = rv_ref[0, pl.ds(rs, CHUNK), :]
        qb = q.astype(_BF)
        sc = _dot_nt(qb, k.astype(_BF)) * dec
        o = _dot(sc.astype(_BF), v) + _dot(qb, state.astype(_BF)) * xi
        mu = jnp.mean(o, axis=-1, keepdims=True)
        d = o - mu
        var = jnp.mean(d * d, axis=-1, keepdims=True)
        y = d * lax.rsqrt(var + GN_EPS) * gw + gb
        o_ref[0, pl.ds(rs, CHUNK), :] = (
            y * zs_ref[0, pl.ds(rs, CHUNK), :].astype(_F32)).astype(_BF)
        return state * gch + _dot_tn((k * zeta).astype(_BF), v)

    lax.fori_loop(0, seq // CHUNK, body, state0)


def _retention(proj3, proj_meta, tabs_x, tabs_m, consts, gn_w, gn_b):
    batch, seq = proj3.shape[0], proj3.shape[1]
    col = lambda off: (lambda b, h: (b, 0, off // RET_D + h))
    mcol = lambda off: (lambda b, h: (0, off // RET_D + h))
    head3 = pl.BlockSpec((1, CHUNK, RET_D), lambda b, h: (h, 0, 0))
    return pl.pallas_call(
        functools.partial(_ret_kernel, seq=seq),
        grid=(batch, HEADS),
        in_specs=[
            pl.BlockSpec((1, seq, RET_D), col(RQ_OFF)),
            pl.BlockSpec((1, seq, RET_D), col(RK_OFF)),
            pl.BlockSpec((1, seq, RET_D), col(RV_OFF)),
            pl.BlockSpec((1, seq, RET_D), col(ZR_OFF)),
            pl.BlockSpec((CHUNK, RET_D), mcol(RK_OFF)),
            pl.BlockSpec((CHUNK, RET_D), mcol(RV_OFF)),
            pl.BlockSpec((seq, RET_D), lambda b, h: (0, 0)),
            pl.BlockSpec((seq, RET_D), lambda b, h: (0, 0)),
            pl.BlockSpec((CHUNK, RET_D), lambda b, h: (0, 0)),
            pl.BlockSpec((CHUNK, RET_D), lambda b, h: (0, 0)),
            head3, head3, head3, head3,
            pl.BlockSpec((1, RET_D), lambda b, h: (0, h)),
            pl.BlockSpec((1, RET_D), lambda b, h: (0, h)),
        ],
        out_specs=pl.BlockSpec((1, seq, RET_D), lambda b, h: (b, 0, h)),
        out_shape=jax.ShapeDtypeStruct((batch, seq, HEADS * RET_D), _BF),
        compiler_params=pltpu.CompilerParams(
            dimension_semantics=("parallel", "parallel"), vmem_limit_bytes=VMEM_LIMIT),
        name="retention",
    )(proj3, proj3, proj3, proj3, proj_meta, proj_meta, *tabs_x, *tabs_m, *consts, gn_w, gn_b)


def _out_proj_kernel(ym_ref, yr_ref, g0_ref, g1_ref, x_ref, wbm_ref, wbr_ref, wo_ref, fw_ref, o_ref):
    m1 = _dot(ym_ref[...], wbm_ref[...])
    m2 = _dot(yr_ref[...], wbr_ref[...])
    merged = g0_ref[...].astype(_F32) * m1 + g1_ref[...].astype(_F32) * m2
    h = x_ref[...] + _dot(merged.astype(_BF), wo_ref[...])
    ms = jnp.mean(h * h, axis=-1, keepdims=True)
    o_ref[...] = h * lax.rsqrt(ms + NORM_EPS) * fw_ref[...]


def _out_proj(y_mla, y_ret, proj, x2d, wbm, wbr, wo, fw, *, tm):
    rows = x2d.shape[0]
    width = HEADS * HEAD_V
    const = lambda shape: pl.BlockSpec(shape, lambda i: (0, 0), pipeline_mode=pl.Buffered(1))
    return pl.pallas_call(
        _out_proj_kernel,
        grid=(rows // tm,),
        in_specs=[
            pl.BlockSpec((tm, width), lambda i: (i, 0)),
            pl.BlockSpec((tm, width), lambda i: (i, 0)),
            pl.BlockSpec((tm, D_MODEL), lambda i: (i, 0)),
            pl.BlockSpec((tm, D_MODEL), lambda i: (i, 1)),
            pl.BlockSpec((tm, D_MODEL), lambda i: (i, 0)),
            const((width, D_MODEL)), const((width, D_MODEL)), const((D_MODEL, D_MODEL)),
            const((1, D_MODEL)),
        ],
        out_specs=pl.BlockSpec((tm, D_MODEL), lambda i: (i, 0)),
        out_shape=jax.ShapeDtypeStruct((rows, D_MODEL), _F32),
        compiler_params=pltpu.CompilerParams(
            dimension_semantics=("parallel",), vmem_limit_bytes=VMEM_LIMIT),
        name="out_proj",
    )(y_mla, y_ret, proj, proj, x2d, wbm, wbr, wo, fw)


def _rope_tables(seq):
    pos = jnp.arange(CHUNK + seq, dtype=_F32) - META_PAD
    z32 = jnp.zeros((pos.shape[0], 32), _F32)
    inv64 = ROPE_BASE ** (-jnp.arange(0, ROPE, 2, dtype=_F32) / ROPE)
    ang = pos[:, None] * inv64[None, :]
    cos, sin = jnp.cos(ang), jnp.sin(ang)
    c64 = jnp.concatenate([cos, cos, z32, z32], axis=1)
    s1 = jnp.concatenate([-sin, z32, z32, z32], axis=1)
    s2 = jnp.concatenate([z32, sin, z32, z32], axis=1)
    inv128 = ROPE_BASE ** (-jnp.arange(0, RET_D, 2, dtype=_F32) / RET_D)
    ang = pos[:, None] * inv128[None, :]
    cos, sin = jnp.cos(ang), jnp.sin(ang)
    c128 = jnp.concatenate([cos, cos], axis=1)
    s128 = jnp.concatenate([-sin, sin], axis=1)
    return c64, s1, s2, c128, s128


def _decay_tables():
    log_g = jnp.log1p(-(2.0 ** (-5.0 - jnp.arange(HEADS, dtype=_F32))))
    n = jnp.arange(CHUNK, dtype=_F32)
    diff = n[:, None] - n[None, :]
    dec = jnp.where(diff >= 0, jnp.exp(log_g[:, None, None] * jnp.maximum(diff, 0.0)), 0.0)
    ones = jnp.ones((1, 1, RET_D), _F32)
    zeta = jnp.exp(log_g[:, None] * (CHUNK - 1.0 - n))[:, :, None] * ones
    xi = jnp.exp(log_g[:, None] * (n + 1.0))[:, :, None] * ones
    gch = jnp.exp(log_g * CHUNK)[:, None, None] * jnp.ones((1, CHUNK, RET_D), _F32)
    return dec, zeta, xi, gch


def _pack_w_in(w_in):
    c_q, c_kv, k_pe, z_mla, r_q, r_k, r_v, z_ret, gates = jnp.split(
        w_in, np.cumsum([512, 256, 64, 1024, 1024, 1024, 1024, 1024]).tolist(), axis=1)
    pad = jnp.zeros((D_MODEL, IN_PAD - KPE_OFF - ROPE), w_in.dtype)
    packed = jnp.concatenate([gates, z_mla, z_ret, r_q, r_k, r_v, c_q, c_kv, k_pe, pad], axis=1)
    return packed.astype(_BF)


def _pack_w_uq(w_uq):
    w = w_uq.reshape(Q_RANK, HEADS, NOPE + ROPE)
    w = jnp.pad(w, ((0, 0), (0, 0), (0, QK_PAD - NOPE - ROPE)))
    return w.reshape(Q_RANK, HEADS * QK_PAD).astype(_BF)


def _split_w_ukv(w_ukv):
    w = w_ukv.reshape(KV_RANK, HEADS, NOPE + HEAD_V)
    wk = w[:, :, :NOPE].reshape(KV_RANK, HEADS * NOPE)
    wv = w[:, :, NOPE:].reshape(KV_RANK, HEADS * HEAD_V)
    return wk.astype(_BF), wv.astype(_BF)


def kernel(x, meta, norm_w, w_in, mla_q_norm_w, mla_w_uq, mla_kv_norm_w, mla_w_ukv, ret_gn_w,
           ret_gn_b, w_branch_mla, w_branch_ret, w_out, final_norm_w):
    batch, seq, _ = x.shape
    rows = batch * seq
    x2d = x.reshape(rows, D_MODEL)
    meta_chunk = jnp.pad(meta.astype(x.dtype), ((META_PAD, 0), (0, 0)))

    w_in_bf = _pack_w_in(w_in[0])
    wq = _pack_w_uq(mla_w_uq[0])
    wk, wv = _split_w_ukv(mla_w_ukv[0])
    qnw, kvnw = mla_q_norm_w, mla_kv_norm_w
    scale = float((NOPE + ROPE) ** -0.5)

    c64, s1, s2, c128, s128 = _rope_tables(seq)
    tabs64_x = (c64[CHUNK:], s1[CHUNK:], s2[CHUNK:])
    tabs64_m = (c64[:CHUNK], s1[:CHUNK], s2[:CHUNK])
    tabs128_x = (c128[CHUNK:], s128[CHUNK:])
    tabs128_m = (c128[:CHUNK], s128[:CHUNK])

    proj = _in_proj(x2d, norm_w, w_in_bf, tm=1024, tn=1024)
    proj_m = _in_proj(meta_chunk, norm_w, w_in_bf, tm=CHUNK, tn=1024)

    q, k, v = _mla_prep(proj, qnw, kvnw, wq, wk, wv, tabs64_x, tm=512, scale=scale)
    _, k_m, v_m = _mla_prep(proj_m, qnw, kvnw, wq, wk, wv, tabs64_m, tm=CHUNK, scale=scale)

    proj3 = proj.reshape(batch, seq, IN_PAD)
    y_mla = _attention(q.reshape(batch, seq, -1), k.reshape(batch, seq, -1),
                       v.reshape(batch, seq, -1), k_m, v_m, proj3, tq=512)
    y_ret = _retention(proj3, proj_m, tabs128_x, tabs128_m, _decay_tables(),
                       ret_gn_w, ret_gn_b)

    out = _out_proj(y_mla.reshape(rows, -1), y_ret.reshape(rows, -1), proj, x2d,
                    w_branch_mla[0].astype(_BF), w_branch_ret[0].astype(_BF),
                    w_out[0].astype(_BF), final_norm_w.reshape(1, D_MODEL), tm=256)
    return out.reshape(batch, seq, D_MODEL)
```

```python
import functools

import numpy as np
import jax
import jax.numpy as jnp
from jax import lax
from jax.experimental import pallas as pl
from jax.experimental.pallas import tpu as pltpu

D_MODEL = 2048
N_META = 16
CHUNK = 128
META_PAD = CHUNK - N_META
HEADS = 8
NOPE = 128
ROPE = 64
HEAD_V = 128
Q_RANK = 512
KV_RANK = 256
QK_PAD = 256
RET_D = 128
ROPE_BASE = 10000.0
NORM_EPS = 1e-6
GN_EPS = 1e-5
NEG_INF = -1e30

GATE_OFF = 0
ZM_OFF = 4096
ZR_OFF = 5120
RQ_OFF = 6144
RK_OFF = 7168
RV_OFF = 8192
CQ_OFF = 9216
CKV_OFF = 9728
KPE_OFF = 9984
IN_PAD = 10240

VMEM_LIMIT = 56 * 1024 * 1024

_BF = jnp.bfloat16
_F32 = jnp.float32


def _dot(a, b):
    return jnp.dot(a, b, preferred_element_type=_F32)


def _dot_nt(a, b):
    return lax.dot_general(a, b, (((1,), (1,)), ((), ())), preferred_element_type=_F32)


def _dot_tn(a, b):
    return lax.dot_general(a, b, (((0,), (0,)), ((), ())), preferred_element_type=_F32)


def _in_proj_kernel(x_ref, nw_ref, w_ref, o_ref, xn_ref, *, tn):
    j = pl.program_id(1)

    @pl.when(j == 0)
    def _():
        x = x_ref[...]
        ms = jnp.mean(x * x, axis=-1, keepdims=True)
        xn_ref[...] = (x * lax.rsqrt(ms + NORM_EPS) * nw_ref[...]).astype(_BF)

    acc = _dot(xn_ref[...], w_ref[...])
    col = j * tn

    @pl.when(col < ZM_OFF)
    def _():
        o_ref[...] = jax.nn.sigmoid(acc).astype(_BF)

    @pl.when((col >= ZM_OFF) & (col < RQ_OFF))
    def _():
        o_ref[...] = (acc * jax.nn.sigmoid(acc)).astype(_BF)

    @pl.when(col >= RQ_OFF)
    def _():
        o_ref[...] = acc.astype(_BF)


def _in_proj(x2d, norm_w, w_in_bf, *, tm, tn):
    rows = x2d.shape[0]
    return pl.pallas_call(
        functools.partial(_in_proj_kernel, tn=tn),
        grid=(rows // tm, IN_PAD // tn),
        in_specs=[
            pl.BlockSpec((tm, D_MODEL), lambda i, j: (i, 0)),
            pl.BlockSpec((1, D_MODEL), lambda i, j: (0, 0)),
            pl.BlockSpec((D_MODEL, tn), lambda i, j: (0, j)),
        ],
        out_specs=pl.BlockSpec((tm, tn), lambda i, j: (i, j)),
        out_shape=jax.ShapeDtypeStruct((rows, IN_PAD), _BF),
        scratch_shapes=[pltpu.VMEM((tm, D_MODEL), _BF)],
        compiler_params=pltpu.CompilerParams(
            dimension_semantics=("parallel", "arbitrary"), vmem_limit_bytes=VMEM_LIMIT),
        name="in_proj",
    )(x2d, norm_w, w_in_bf)


def _rms(x, w):
    ms = jnp.mean(x * x, axis=-1, keepdims=True)
    return x * lax.rsqrt(ms + NORM_EPS) * w


def _rope64(t, c, s1, s2):
    return t * c + pltpu.roll(t, 96, 1) * s1 + pltpu.roll(t, 32, 1) * s2


def _mla_prep_kernel(cq_ref, ckv_ref, kpe_ref, qnw_ref, kvnw_ref, wq_ref, wk_ref, wvt_ref,
                     c_ref, s1_ref, s2_ref, q_ref, k_ref, vt_ref, *, scale):
    c, s1, s2 = c_ref[...], s1_ref[...], s2_ref[...]

    cqn = _rms(cq_ref[...].astype(_F32), qnw_ref[...]).astype(_BF)
    q = _dot(cqn, wq_ref[...])
    for h in range(HEADS):
        lo = h * QK_PAD
        q_ref[:, lo:lo + NOPE] = (q[:, lo:lo + NOPE] * scale).astype(_BF)
        q_ref[:, lo + NOPE:lo + QK_PAD] = (
            _rope64(q[:, lo + NOPE:lo + QK_PAD], c, s1, s2) * scale).astype(_BF)

    ckvn = _rms(ckv_ref[...].astype(_F32), kvnw_ref[...]).astype(_BF)
    kn = _dot(ckvn, wk_ref[...])
    kpe = _rope64(kpe_ref[...].astype(_F32), c, s1, s2).astype(_BF)
    for h in range(HEADS):
        lo = h * QK_PAD
        k_ref[:, lo:lo + NOPE] = kn[:, h * NOPE:(h + 1) * NOPE].astype(_BF)
        k_ref[:, lo + NOPE:lo + QK_PAD] = kpe
    vt_ref[0] = _dot_nt(wvt_ref[...], ckvn).astype(_BF)


def _mla_prep(proj, qnw, kvnw, wq, wk, wvt, tabs, *, tm, scale):
    rows = proj.shape[0]
    seq = tabs[0].shape[0]
    tab_blocks = seq // tm
    full = lambda shape: pl.BlockSpec(shape, lambda i: (0, 0))
    tab_spec = pl.BlockSpec((tm, 128), lambda i: (i % tab_blocks, 0))
    return pl.pallas_call(
        functools.partial(_mla_prep_kernel, scale=scale),
        grid=(rows // tm,),
        in_specs=[
            pl.BlockSpec((tm, Q_RANK), lambda i: (i, CQ_OFF // Q_RANK)),
            pl.BlockSpec((tm, KV_RANK), lambda i: (i, CKV_OFF // KV_RANK)),
            pl.BlockSpec((tm, 128), lambda i: (i, KPE_OFF // 128)),
            full((1, Q_RANK)), full((1, KV_RANK)),
            full((Q_RANK, HEADS * QK_PAD)), full((KV_RANK, HEADS * NOPE)),
            full((HEADS * HEAD_V, KV_RANK)),
            tab_spec, tab_spec, tab_spec,
        ],
        out_specs=[
            pl.BlockSpec((tm, HEADS * QK_PAD), lambda i: (i, 0)),
            pl.BlockSpec((tm, HEADS * QK_PAD), lambda i: (i, 0)),
            pl.BlockSpec((1, HEADS * HEAD_V, tm), lambda i: (i // tab_blocks, 0, i % tab_blocks)),
        ],
        out_shape=[
            jax.ShapeDtypeStruct((rows, HEADS * QK_PAD), _BF),
            jax.ShapeDtypeStruct((rows, HEADS * QK_PAD), _BF),
            jax.ShapeDtypeStruct((rows // seq, HEADS * HEAD_V, seq), _BF),
        ],
        compiler_params=pltpu.CompilerParams(
            dimension_semantics=("parallel",), vmem_limit_bytes=VMEM_LIMIT),
        name="mla_prep",
    )(proj, proj, proj, qnw, kvnw, wq, wk, wvt, *tabs)


def _attn_kernel(q_ref, k_ref, vt_ref, km_ref, vmt_ref, zs_ref, o_ref, m_ref, l_ref, acc_ref,
                 sa_ref, sb_ref, *, seq, tq):
    def update(s, vt, mask):
        if mask is not None:
            s = jnp.where(mask, s, NEG_INF)
        m_old = m_ref[...]
        m_new = jnp.maximum(m_old, jnp.max(s, axis=0, keepdims=True))
        alpha = jnp.exp2(m_old - m_new)
        p = jnp.exp2(s - m_new)
        l_ref[...] = alpha * l_ref[...] + jnp.sum(p, axis=0, keepdims=True)
        acc_ref[...] = alpha * acc_ref[...] + _dot(vt, p.astype(_BF))
        m_ref[...] = m_new

    meta_mask = lax.broadcasted_iota(jnp.int32, (CHUNK, tq), 0) >= META_PAD
    key = lax.broadcasted_iota(jnp.int32, (tq, tq), 0)
    qry = lax.broadcasted_iota(jnp.int32, (tq, tq), 1)
    causal = key <= qry

    for qi in range(seq // tq):
        rows = slice(qi * tq, (qi + 1) * tq)
        q = q_ref[0, rows, :]
        m_ref[...] = jnp.full((1, tq), NEG_INF, _F32)
        l_ref[...] = jnp.zeros((1, tq), _F32)
        acc_ref[...] = jnp.zeros((HEAD_V, tq), _F32)

        def scores(blk):
            return _dot_nt(k_ref[0, pl.ds(pl.multiple_of(blk * tq, tq), tq), :], q)

        def values(blk):
            return vt_ref[0, :, pl.ds(pl.multiple_of(blk * tq, tq), tq)]

        sa_ref[...] = scores(0)
        update(_dot_nt(km_ref[...], q), vmt_ref[...], meta_mask)

        def pair(i, carry):
            sb_ref[...] = scores(2 * i + 1)
            update(sa_ref[...], values(2 * i), None)
            sa_ref[...] = scores(2 * i + 2)
            update(sb_ref[...], values(2 * i + 1), None)
            return carry

        if qi >= 2:
            lax.fori_loop(0, qi // 2, pair, 0)
        if qi % 2 == 1:
            sb_ref[...] = scores(qi)
            update(sa_ref[...], values(qi - 1), None)
            update(sb_ref[...], values(qi), causal)
        else:
            update(sa_ref[...], values(qi), causal)

        y = acc_ref[...] * (1.0 / l_ref[...])
        o_ref[0, rows, :] = (y.T * zs_ref[0, rows, :].astype(_F32)).astype(_BF)


def _attention(q, k, vt, k_meta, vt_meta, proj3, *, tq):
    batch, seq = q.shape[0], q.shape[1]
    return pl.pallas_call(
        functools.partial(_attn_kernel, seq=seq, tq=tq),
        grid=(batch, HEADS),
        in_specs=[
            pl.BlockSpec((1, seq, QK_PAD), lambda b, h: (b, 0, h)),
            pl.BlockSpec((1, seq, QK_PAD), lambda b, h: (b, 0, h)),
            pl.BlockSpec((1, HEAD_V, seq), lambda b, h: (b, h, 0)),
            pl.BlockSpec((CHUNK, QK_PAD), lambda b, h: (0, h)),
            pl.BlockSpec((HEAD_V, CHUNK), lambda b, h: (h, 0)),
            pl.BlockSpec((1, seq, HEAD_V), lambda b, h: (b, 0, ZM_OFF // HEAD_V + h)),
        ],
        out_specs=pl.BlockSpec((1, seq, HEAD_V), lambda b, h: (b, 0, h)),
        out_shape=jax.ShapeDtypeStruct((batch, seq, HEADS * HEAD_V), _BF),
        scratch_shapes=[pltpu.VMEM((1, tq), _F32), pltpu.VMEM((1, tq), _F32),
                        pltpu.VMEM((HEAD_V, tq), _F32),
                        pltpu.VMEM((tq, tq), _F32), pltpu.VMEM((tq, tq), _F32)],
        compiler_params=pltpu.CompilerParams(
            dimension_semantics=("parallel", "parallel"), vmem_limit_bytes=VMEM_LIMIT),
        name="attention",
    )(q, k, vt, k_meta, vt_meta, proj3)


def _rope128(t, c, s):
    return t * c + pltpu.roll(t, 64, 1) * s


def _ret_kernel(rq_ref, rk_ref, rv_ref, zs_ref, rkm_ref, rvm_ref, c_ref, s_ref, cm_ref, sm_ref,
                dec_ref, zeta_ref, xi_ref, gch_ref, gw_ref, gb_ref, o_ref, *, seq):
    kscale = RET_D ** -0.5
    dec = dec_ref[0]
    zeta = zeta_ref[0]
    xi = xi_ref[0]
    gch = gch_ref[0]
    gw = gw_ref[...]
    gb = gb_ref[...]

    km = _rope128(rkm_ref[...].astype(_F32), cm_ref[...], sm_ref[...]) * kscale
    state0 = _dot_tn((km * zeta).astype(_BF), rvm_ref[...])

    def body(ci, state):
        rs = pl.multiple_of(ci * CHUNK, CHUNK)
        c = c_ref[pl.ds(rs, CHUNK), :]
        s = s_ref[pl.ds(rs, CHUNK), :]
        q = _rope128(rq_ref[0, pl.ds(rs, CHUNK), :].astype(_F32), c, s)
        k = _rope128(rk_ref[0, pl.ds(rs, CHUNK), :].astype(_F32), c, s) * kscale
        v = rv_ref[0, pl.ds(rs, CHUNK), :]
        qb = q.astype(_BF)
        sc = _dot_nt(qb, k.astype(_BF)) * dec
        o = _dot(sc.astype(_BF), v) + _dot(qb, state.astype(_BF)) * xi
        mu = jnp.mean(o, axis=-1, keepdims=True)
        d = o - mu
        var = jnp.mean(d * d, axis=-1, keepdims=True)
        y = d * lax.rsqrt(var + GN_EPS) * gw + gb
        o_ref[0, pl.ds(rs, CHUNK), :] = (
            y * zs_ref[0, pl.ds(rs, CHUNK), :].astype(_F32)).astype(_BF)
        return state * gch + _dot_tn((k * zeta).astype(_BF), v)

    lax.fori_loop(0, seq // CHUNK, body, state0)


def _retention(proj3, proj_meta, tabs_x, tabs_m, consts, gn_w, gn_b):
    batch, seq = proj3.shape[0], proj3.shape[1]
    col = lambda off: (lambda b, h: (b, 0, off // RET_D + h))
    mcol = lambda off: (lambda b, h: (0, off // RET_D + h))
    head3 = pl.BlockSpec((1, CHUNK, RET_D), lambda b, h: (h, 0, 0))
    return pl.pallas_call(
        functools.partial(_ret_kernel, seq=seq),
        grid=(batch, HEADS),
        in_specs=[
            pl.BlockSpec((1, seq, RET_D), col(RQ_OFF)),
            pl.BlockSpec((1, seq, RET_D), col(RK_OFF)),
            pl.BlockSpec((1, seq, RET_D), col(RV_OFF)),
            pl.BlockSpec((1, seq, RET_D), col(ZR_OFF)),
            pl.BlockSpec((CHUNK, RET_D), mcol(RK_OFF)),
            pl.BlockSpec((CHUNK, RET_D), mcol(RV_OFF)),
            pl.BlockSpec((seq, RET_D), lambda b, h: (0, 0)),
            pl.BlockSpec((seq, RET_D), lambda b, h: (0, 0)),
            pl.BlockSpec((CHUNK, RET_D), lambda b, h: (0, 0)),
            pl.BlockSpec((CHUNK, RET_D), lambda b, h: (0, 0)),
            head3, head3, head3, head3,
            pl.BlockSpec((1, RET_D), lambda b, h: (0, h)),
            pl.BlockSpec((1, RET_D), lambda b, h: (0, h)),
        ],
        out_specs=pl.BlockSpec((1, seq, RET_D), lambda b, h: (b, 0, h)),
        out_shape=jax.ShapeDtypeStruct((batch, seq, HEADS * RET_D), _BF),
        compiler_params=pltpu.CompilerParams(
            dimension_semantics=("parallel", "parallel"), vmem_limit_bytes=VMEM_LIMIT),
        name="retention",
    )(proj3, proj3, proj3, proj3, proj_meta, proj_meta, *tabs_x, *tabs_m, *consts, gn_w, gn_b)


def _out_proj_kernel(ym_ref, yr_ref, g0_ref, g1_ref, x_ref, wbm_ref, wbr_ref, wo_ref, fw_ref, o_ref):
    m1 = _dot(ym_ref[...], wbm_ref[...])
    m2 = _dot(yr_ref[...], wbr_ref[...])
    merged = g0_ref[...].astype(_F32) * m1 + g1_ref[...].astype(_F32) * m2
    h = x_ref[...] + _dot(merged.astype(_BF), wo_ref[...])
    ms = jnp.mean(h * h, axis=-1, keepdims=True)
    o_ref[...] = h * lax.rsqrt(ms + NORM_EPS) * fw_ref[...]


def _out_proj(y_mla, y_ret, proj, x2d, wbm, wbr, wo, fw, *, tm):
    rows = x2d.shape[0]
    width = HEADS * HEAD_V
    const = lambda shape: pl.BlockSpec(shape, lambda i: (0, 0), pipeline_mode=pl.Buffered(1))
    return pl.pallas_call(
        _out_proj_kernel,
        grid=(rows // tm,),
        in_specs=[
            pl.BlockSpec((tm, width), lambda i: (i, 0)),
            pl.BlockSpec((tm, width), lambda i: (i, 0)),
            pl.BlockSpec((tm, D_MODEL), lambda i: (i, 0)),
            pl.BlockSpec((tm, D_MODEL), lambda i: (i, 1)),
            pl.BlockSpec((tm, D_MODEL), lambda i: (i, 0)),
            const((width, D_MODEL)), const((width, D_MODEL)), const((D_MODEL, D_MODEL)),
            const((1, D_MODEL)),
        ],
        out_specs=pl.BlockSpec((tm, D_MODEL), lambda i: (i, 0)),
        out_shape=jax.ShapeDtypeStruct((rows, D_MODEL), _F32),
        compiler_params=pltpu.CompilerParams(
            dimension_semantics=("parallel",), vmem_limit_bytes=VMEM_LIMIT),
        name="out_proj",
    )(y_mla, y_ret, proj, proj, x2d, wbm, wbr, wo, fw)


def _rope_tables(seq):
    pos = jnp.arange(CHUNK + seq, dtype=_F32) - META_PAD
    z32 = jnp.zeros((pos.shape[0], 32), _F32)
    inv64 = ROPE_BASE ** (-jnp.arange(0, ROPE, 2, dtype=_F32) / ROPE)
    ang = pos[:, None] * inv64[None, :]
    cos, sin = jnp.cos(ang), jnp.sin(ang)
    c64 = jnp.concatenate([cos, cos, z32, z32], axis=1)
    s1 = jnp.concatenate([-sin, z32, z32, z32], axis=1)
    s2 = jnp.concatenate([z32, sin, z32, z32], axis=1)
    inv128 = ROPE_BASE ** (-jnp.arange(0, RET_D, 2, dtype=_F32) / RET_D)
    ang = pos[:, None] * inv128[None, :]
    cos, sin = jnp.cos(ang), jnp.sin(ang)
    c128 = jnp.concatenate([cos, cos], axis=1)
    s128 = jnp.concatenate([-sin, sin], axis=1)
    return c64, s1, s2, c128, s128


def _decay_tables():
    log_g = jnp.log1p(-(2.0 ** (-5.0 - jnp.arange(HEADS, dtype=_F32))))
    n = jnp.arange(CHUNK, dtype=_F32)
    diff = n[:, None] - n[None, :]
    dec = jnp.where(diff >= 0, jnp.exp(log_g[:, None, None] * jnp.maximum(diff, 0.0)), 0.0)
    ones = jnp.ones((1, 1, RET_D), _F32)
    zeta = jnp.exp(log_g[:, None] * (CHUNK - 1.0 - n))[:, :, None] * ones
    xi = jnp.exp(log_g[:, None] * (n + 1.0))[:, :, None] * ones
    gch = jnp.exp(log_g * CHUNK)[:, None, None] * jnp.ones((1, CHUNK, RET_D), _F32)
    return dec, zeta, xi, gch


def _pack_w_in(w_in):
    c_q, c_kv, k_pe, z_mla, r_q, r_k, r_v, z_ret, gates = jnp.split(
        w_in, np.cumsum([512, 256, 64, 1024, 1024, 1024, 1024, 1024]).tolist(), axis=1)
    pad = jnp.zeros((D_MODEL, IN_PAD - KPE_OFF - ROPE), w_in.dtype)
    packed = jnp.concatenate([gates, z_mla, z_ret, r_q, r_k, r_v, c_q, c_kv, k_pe, pad], axis=1)
    return packed.astype(_BF)


def _pack_w_uq(w_uq):
    w = w_uq.reshape(Q_RANK, HEADS, NOPE + ROPE)
    w = jnp.pad(w, ((0, 0), (0, 0), (0, QK_PAD - NOPE - ROPE)))
    return w.reshape(Q_RANK, HEADS * QK_PAD).astype(_BF)


def _split_w_ukv(w_ukv):
    w = w_ukv.reshape(KV_RANK, HEADS, NOPE + HEAD_V)
    wk = w[:, :, :NOPE].reshape(KV_RANK, HEADS * NOPE)
    wvt = w[:, :, NOPE:].reshape(KV_RANK, HEADS * HEAD_V).T
    return wk.astype(_BF), wvt.astype(_BF)


def kernel(x, meta, norm_w, w_in, mla_q_norm_w, mla_w_uq, mla_kv_norm_w, mla_w_ukv, ret_gn_w,
           ret_gn_b, w_branch_mla, w_branch_ret, w_out, final_norm_w):
    batch, seq, _ = x.shape
    rows = batch * seq
    x2d = x.reshape(rows, D_MODEL)
    meta_chunk = jnp.pad(meta.astype(x.dtype), ((META_PAD, 0), (0, 0)))

    w_in_bf = _pack_w_in(w_in[0])
    wq = _pack_w_uq(mla_w_uq[0])
    wk, wvt = _split_w_ukv(mla_w_ukv[0])
    qnw, kvnw = mla_q_norm_w, mla_kv_norm_w
    scale = float((NOPE + ROPE) ** -0.5 * np.log2(np.e))

    c64, s1, s2, c128, s128 = _rope_tables(seq)
    tabs64_x = (c64[CHUNK:], s1[CHUNK:], s2[CHUNK:])
    tabs64_m = (c64[:CHUNK], s1[:CHUNK], s2[:CHUNK])
    tabs128_x = (c128[CHUNK:], s128[CHUNK:])
    tabs128_m = (c128[:CHUNK], s128[:CHUNK])

    proj = _in_proj(x2d, norm_w, w_in_bf, tm=1024, tn=1024)
    proj_m = _in_proj(meta_chunk, norm_w, w_in_bf, tm=CHUNK, tn=1024)

    q, k, vt = _mla_prep(proj, qnw, kvnw, wq, wk, wvt, tabs64_x, tm=512, scale=scale)
    _, k_m, vt_m = _mla_prep(proj_m, qnw, kvnw, wq, wk, wvt, tabs64_m, tm=CHUNK, scale=scale)

    proj3 = proj.reshape(batch, seq, IN_PAD)
    y_mla = _attention(q.reshape(batch, seq, -1), k.reshape(batch, seq, -1), vt,
                       k_m, vt_m[0], proj3, tq=512)
    y_ret = _retention(proj3, proj_m, tabs128_x, tabs128_m, _decay_tables(),
                       ret_gn_w, ret_gn_b)

    out = _out_proj(y_mla.reshape(rows, -1), y_ret.reshape(rows, -1), proj, x2d,
                    w_branch_mla[0].astype(_BF), w_branch_ret[0].astype(_BF),
                    w_out[0].astype(_BF), final_norm_w.reshape(1, D_MODEL), tm=256)
    return out.reshape(batch, seq, D_MODEL)
```

```python
import functools

import numpy as np
import jax
import jax.numpy as jnp
from jax import lax
from jax.experimental import pallas as pl
from jax.experimental.pallas import tpu as pltpu

D_MODEL = 2048
N_META = 16
CHUNK = 128
META_PAD = CHUNK - N_META
HEADS = 8
NOPE = 128
ROPE = 64
HEAD_V = 128
Q_RANK = 512
KV_RANK = 256
QK_PAD = 256
RET_D = 128
ROPE_BASE = 10000.0
NORM_EPS = 1e-6
GN_EPS = 1e-5
NEG_INF = -1e30

GATE_OFF = 0
ZM_OFF = 4096
ZR_OFF = 5120
RQ_OFF = 6144
RK_OFF = 7168
RV_OFF = 8192
CQ_OFF = 9216
CKV_OFF = 9728
KPE_OFF = 9984
IN_PAD = 10240

VMEM_LIMIT = 56 * 1024 * 1024

_BF = jnp.bfloat16
_F32 = jnp.float32


def _dot(a, b):
    return jnp.dot(a, b, preferred_element_type=_F32)


def _dot_nt(a, b):
    return lax.dot_general(a, b, (((1,), (1,)), ((), ())), preferred_element_type=_F32)


def _dot_tn(a, b):
    return lax.dot_general(a, b, (((0,), (0,)), ((), ())), preferred_element_type=_F32)


def _in_proj_kernel(x_ref, nw_ref, w_ref, o_ref, xn_ref, *, tn):
    j = pl.program_id(1)

    @pl.when(j == 0)
    def _():
        x = x_ref[...]
        ms = jnp.mean(x * x, axis=-1, keepdims=True)
        xn_ref[...] = (x * lax.rsqrt(ms + NORM_EPS) * nw_ref[...]).astype(_BF)

    acc = _dot(xn_ref[...], w_ref[...])
    col = j * tn

    @pl.when(col < ZM_OFF)
    def _():
        o_ref[...] = jax.nn.sigmoid(acc).astype(_BF)

    @pl.when((col >= ZM_OFF) & (col < RQ_OFF))
    def _():
        o_ref[...] = (acc * jax.nn.sigmoid(acc)).astype(_BF)

    @pl.when(col >= RQ_OFF)
    def _():
        o_ref[...] = acc.astype(_BF)


def _in_proj(x2d, norm_w, w_in_bf, *, tm, tn):
    rows = x2d.shape[0]
    return pl.pallas_call(
        functools.partial(_in_proj_kernel, tn=tn),
        grid=(rows // tm, IN_PAD // tn),
        in_specs=[
            pl.BlockSpec((tm, D_MODEL), lambda i, j: (i, 0)),
            pl.BlockSpec((1, D_MODEL), lambda i, j: (0, 0)),
            pl.BlockSpec((D_MODEL, tn), lambda i, j: (0, j)),
        ],
        out_specs=pl.BlockSpec((tm, tn), lambda i, j: (i, j)),
        out_shape=jax.ShapeDtypeStruct((rows, IN_PAD), _BF),
        scratch_shapes=[pltpu.VMEM((tm, D_MODEL), _BF)],
        compiler_params=pltpu.CompilerParams(
            dimension_semantics=("parallel", "arbitrary"), vmem_limit_bytes=VMEM_LIMIT),
        name="in_proj",
    )(x2d, norm_w, w_in_bf)


def _rms(x, w):
    ms = jnp.mean(x * x, axis=-1, keepdims=True)
    return x * lax.rsqrt(ms + NORM_EPS) * w


def _rope64(t, c, s1, s2):
    return t * c + pltpu.roll(t, 96, 1) * s1 + pltpu.roll(t, 32, 1) * s2


def _mla_prep_kernel(cq_ref, ckv_ref, kpe_ref, qnw_ref, kvnw_ref, wq_ref, wk_ref, wvt_ref,
                     c_ref, s1_ref, s2_ref, q_ref, k_ref, vt_ref, *, scale):
    c, s1, s2 = c_ref[...], s1_ref[...], s2_ref[...]

    cqn = _rms(cq_ref[...].astype(_F32), qnw_ref[...]).astype(_BF)
    q = _dot(cqn, wq_ref[...])
    for h in range(HEADS):
        lo = h * QK_PAD
        q_ref[:, lo:lo + NOPE] = (q[:, lo:lo + NOPE] * scale).astype(_BF)
        q_ref[:, lo + NOPE:lo + QK_PAD] = (
            _rope64(q[:, lo + NOPE:lo + QK_PAD], c, s1, s2) * scale).astype(_BF)

    ckvn = _rms(ckv_ref[...].astype(_F32), kvnw_ref[...]).astype(_BF)
    kn = _dot(ckvn, wk_ref[...])
    kpe = _rope64(kpe_ref[...].astype(_F32), c, s1, s2).astype(_BF)
    for h in range(HEADS):
        lo = h * QK_PAD
        k_ref[:, lo:lo + NOPE] = kn[:, h * NOPE:(h + 1) * NOPE].astype(_BF)
        k_ref[:, lo + NOPE:lo + QK_PAD] = kpe
    vt_ref[0] = _dot_nt(wvt_ref[...], ckvn).astype(_BF)


def _mla_prep(proj, qnw, kvnw, wq, wk, wvt, tabs, *, tm, scale):
    rows = proj.shape[0]
    seq = tabs[0].shape[0]
    tab_blocks = seq // tm
    full = lambda shape: pl.BlockSpec(shape, lambda i: (0, 0))
    tab_spec = pl.BlockSpec((tm, 128), lambda i: (i % tab_blocks, 0))
    return pl.pallas_call(
        functools.partial(_mla_prep_kernel, scale=scale),
        grid=(rows // tm,),
        in_specs=[
            pl.BlockSpec((tm, Q_RANK), lambda i: (i, CQ_OFF // Q_RANK)),
            pl.BlockSpec((tm, KV_RANK), lambda i: (i, CKV_OFF // KV_RANK)),
            pl.BlockSpec((tm, 128), lambda i: (i, KPE_OFF // 128)),
            full((1, Q_RANK)), full((1, KV_RANK)),
            full((Q_RANK, HEADS * QK_PAD)), full((KV_RANK, HEADS * NOPE)),
            full((HEADS * HEAD_V, KV_RANK)),
            tab_spec, tab_spec, tab_spec,
        ],
        out_specs=[
            pl.BlockSpec((tm, HEADS * QK_PAD), lambda i: (i, 0)),
            pl.BlockSpec((tm, HEADS * QK_PAD), lambda i: (i, 0)),
            pl.BlockSpec((1, HEADS * HEAD_V, tm), lambda i: (i // tab_blocks, 0, i % tab_blocks)),
        ],
        out_shape=[
            jax.ShapeDtypeStruct((rows, HEADS * QK_PAD), _BF),
            jax.ShapeDtypeStruct((rows, HEADS * QK_PAD), _BF),
            jax.ShapeDtypeStruct((rows // seq, HEADS * HEAD_V, seq), _BF),
        ],
        compiler_params=pltpu.CompilerParams(
            dimension_semantics=("parallel",), vmem_limit_bytes=VMEM_LIMIT),
        name="mla_prep",
    )(proj, proj, proj, qnw, kvnw, wq, wk, wvt, *tabs)


def _attn_kernel(q_ref, k_ref, vt_ref, km_ref, vmt_ref, zs_ref, o_ref, m_ref, l_ref, acc_ref,
                 sa_ref, sb_ref, ma_ref, mb_ref, *, seq, tq):
    def update(s, smax, vt):
        m_old = m_ref[...]
        m_new = jnp.maximum(m_old, smax)
        alpha = jnp.exp2(m_old - m_new)
        p = jnp.exp2(s - m_new)
        l_ref[...] = alpha * l_ref[...] + jnp.sum(p, axis=0, keepdims=True)
        acc_ref[...] = alpha * acc_ref[...] + _dot(vt, p.astype(_BF))
        m_ref[...] = m_new

    meta_mask = lax.broadcasted_iota(jnp.int32, (CHUNK, tq), 0) >= META_PAD
    key = lax.broadcasted_iota(jnp.int32, (tq, tq), 0)
    qry = lax.broadcasted_iota(jnp.int32, (tq, tq), 1)
    causal = key <= qry

    for qi in range(seq // tq):
        rows = slice(qi * tq, (qi + 1) * tq)
        q = q_ref[0, rows, :]
        m_ref[...] = jnp.full((1, tq), NEG_INF, _F32)
        l_ref[...] = jnp.zeros((1, tq), _F32)
        acc_ref[...] = jnp.zeros((HEAD_V, tq), _F32)

        def produce(buf, blk, mask=None):
            s_ref, smax_ref = buf
            s = _dot_nt(k_ref[0, pl.ds(pl.multiple_of(blk * tq, tq), tq), :], q)
            if mask is not None:
                s = jnp.where(mask, s, NEG_INF)
            s_ref[...] = s
            smax_ref[...] = jnp.max(s, axis=0, keepdims=True)

        def consume(buf, blk):
            s_ref, smax_ref = buf
            update(s_ref[...], smax_ref[...],
                   vt_ref[0, :, pl.ds(pl.multiple_of(blk * tq, tq), tq)])

        buf_a, buf_b = (sa_ref, ma_ref), (sb_ref, mb_ref)
        produce(buf_a, 0, causal if qi == 0 else None)
        s_meta = jnp.where(meta_mask, _dot_nt(km_ref[...], q), NEG_INF)
        update(s_meta, jnp.max(s_meta, axis=0, keepdims=True), vmt_ref[...])

        def pair(i, carry):
            produce(buf_b, 2 * i + 1)
            consume(buf_a, 2 * i)
            produce(buf_a, 2 * i + 2)
            consume(buf_b, 2 * i + 1)
            return carry

        if qi >= 3:
            lax.fori_loop(0, (qi - 1) // 2, pair, 0)
        done = 2 * ((qi - 1) // 2) if qi >= 3 else 0
        for blk in range(done, qi):
            produce(buf_b if blk % 2 == 0 else buf_a, blk + 1, causal if blk + 1 == qi else None)
            consume(buf_a if blk % 2 == 0 else buf_b, blk)
        consume(buf_a if qi % 2 == 0 else buf_b, qi)

        y = acc_ref[...] * (1.0 / l_ref[...])
        o_ref[0, rows, :] = (y.T * zs_ref[0, rows, :].astype(_F32)).astype(_BF)


def _attention(q, k, vt, k_meta, vt_meta, proj3, *, tq):
    batch, seq = q.shape[0], q.shape[1]
    return pl.pallas_call(
        functools.partial(_attn_kernel, seq=seq, tq=tq),
        grid=(batch, HEADS),
        in_specs=[
            pl.BlockSpec((1, seq, QK_PAD), lambda b, h: (b, 0, h)),
            pl.BlockSpec((1, seq, QK_PAD), lambda b, h: (b, 0, h)),
            pl.BlockSpec((1, HEAD_V, seq), lambda b, h: (b, h, 0)),
            pl.BlockSpec((CHUNK, QK_PAD), lambda b, h: (0, h)),
            pl.BlockSpec((HEAD_V, CHUNK), lambda b, h: (h, 0)),
            pl.BlockSpec((1, seq, HEAD_V), lambda b, h: (b, 0, ZM_OFF // HEAD_V + h)),
        ],
        out_specs=pl.BlockSpec((1, seq, HEAD_V), lambda b, h: (b, 0, h)),
        out_shape=jax.ShapeDtypeStruct((batch, seq, HEADS * HEAD_V), _BF),
        scratch_shapes=[pltpu.VMEM((1, tq), _F32), pltpu.VMEM((1, tq), _F32),
                        pltpu.VMEM((HEAD_V, tq), _F32),
                        pltpu.VMEM((tq, tq), _F32), pltpu.VMEM((tq, tq), _F32),
                        pltpu.VMEM((1, tq), _F32), pltpu.VMEM((1, tq), _F32)],
        compiler_params=pltpu.CompilerParams(
            dimension_semantics=("parallel", "parallel"), vmem_limit_bytes=VMEM_LIMIT),
        name="attention",
    )(q, k, vt, k_meta, vt_meta, proj3)


def _rope128(t, c, s):
    return t * c + pltpu.roll(t, 64, 1) * s


def _ret_kernel(rq_ref, rk_ref, rv_ref, zs_ref, rkm_ref, rvm_ref, c_ref, s_ref, cm_ref, sm_ref,
                dec_ref, zeta_ref, xi_ref, gch_ref, gw_ref, gb_ref, o_ref, *, seq, group):
    kscale = RET_D ** -0.5
    dec = dec_ref[0]
    zeta = zeta_ref[0]
    xi = xi_ref[0]
    gch = gch_ref[0]
    gw = gw_ref[...]
    gb = gb_ref[...]

    km = _rope128(rkm_ref[...].astype(_F32), cm_ref[...], sm_ref[...]) * kscale
    state0 = _dot_tn((km * zeta).astype(_BF), rvm_ref[...])

    rows = group * CHUNK
    bdims = ((0,), (0,))

    def body(gi, state):
        rs = pl.multiple_of(gi * rows, rows)
        c = c_ref[pl.ds(rs, rows), :]
        s = s_ref[pl.ds(rs, rows), :]
        q = _rope128(rq_ref[0, pl.ds(rs, rows), :].astype(_F32), c, s)
        k = _rope128(rk_ref[0, pl.ds(rs, rows), :].astype(_F32), c, s) * kscale
        q3 = q.reshape(group, CHUNK, RET_D)
        k3 = k.reshape(group, CHUNK, RET_D)
        v3 = rv_ref[0, pl.ds(rs, rows), :].reshape(group, CHUNK, RET_D)
        kz = jnp.swapaxes((k3 * zeta[None]).astype(_BF), 1, 2)
        kv = lax.dot_general(kz, v3, (((2,), (1,)), bdims), preferred_element_type=_F32)
        qb = q3.astype(_BF)
        sc = lax.dot_general(qb, k3.astype(_BF), (((2,), (2,)), bdims),
                             preferred_element_type=_F32) * dec[None]
        states = []
        for ci in range(group):
            states.append(state.astype(_BF))
            state = state * gch + kv[ci]
        o = (lax.dot_general(sc.astype(_BF), v3, (((2,), (1,)), bdims),
                             preferred_element_type=_F32)
             + lax.dot_general(qb, jnp.stack(states), (((2,), (1,)), bdims),
                               preferred_element_type=_F32) * xi[None])
        o = o.reshape(rows, RET_D)
        mu = jnp.mean(o, axis=-1, keepdims=True)
        d = o - mu
        var = jnp.mean(d * d, axis=-1, keepdims=True)
        y = d * lax.rsqrt(var + GN_EPS) * gw + gb
        o_ref[0, pl.ds(rs, rows), :] = (
            y * zs_ref[0, pl.ds(rs, rows), :].astype(_F32)).astype(_BF)
        return state

    lax.fori_loop(0, seq // rows, body, state0)


def _retention(proj3, proj_meta, tabs_x, tabs_m, consts, gn_w, gn_b, *, group):
    batch, seq = proj3.shape[0], proj3.shape[1]
    col = lambda off: (lambda b, h: (b, 0, off // RET_D + h))
    mcol = lambda off: (lambda b, h: (0, off // RET_D + h))
    head3 = pl.BlockSpec((1, CHUNK, RET_D), lambda b, h: (h, 0, 0))
    return pl.pallas_call(
        functools.partial(_ret_kernel, seq=seq, group=group),
        grid=(batch, HEADS),
        in_specs=[
            pl.BlockSpec((1, seq, RET_D), col(RQ_OFF)),
            pl.BlockSpec((1, seq, RET_D), col(RK_OFF)),
            pl.BlockSpec((1, seq, RET_D), col(RV_OFF)),
            pl.BlockSpec((1, seq, RET_D), col(ZR_OFF)),
            pl.BlockSpec((CHUNK, RET_D), mcol(RK_OFF)),
            pl.BlockSpec((CHUNK, RET_D), mcol(RV_OFF)),
            pl.BlockSpec((seq, RET_D), lambda b, h: (0, 0)),
            pl.BlockSpec((seq, RET_D), lambda b, h: (0, 0)),
            pl.BlockSpec((CHUNK, RET_D), lambda b, h: (0, 0)),
            pl.BlockSpec((CHUNK, RET_D), lambda b, h: (0, 0)),
            head3, head3, head3, head3,
            pl.BlockSpec((1, RET_D), lambda b, h: (0, h)),
            pl.BlockSpec((1, RET_D), lambda b, h: (0, h)),
        ],
        out_specs=pl.BlockSpec((1, seq, RET_D), lambda b, h: (b, 0, h)),
        out_shape=jax.ShapeDtypeStruct((batch, seq, HEADS * RET_D), _BF),
        compiler_params=pltpu.CompilerParams(
            dimension_semantics=("parallel", "parallel"), vmem_limit_bytes=VMEM_LIMIT),
        name="retention",
    )(proj3, proj3, proj3, proj3, proj_meta, proj_meta, *tabs_x, *tabs_m, *consts, gn_w, gn_b)


def _out_proj_kernel(ym_ref, yr_ref, g0_ref, g1_ref, x_ref, wbm_ref, wbr_ref, wo_ref, fw_ref, o_ref):
    m1 = _dot(ym_ref[...], wbm_ref[...])
    m2 = _dot(yr_ref[...], wbr_ref[...])
    merged = g0_ref[...].astype(_F32) * m1 + g1_ref[...].astype(_F32) * m2
    h = x_ref[...] + _dot(merged.astype(_BF), wo_ref[...])
    ms = jnp.mean(h * h, axis=-1, keepdims=True)
    o_ref[...] = h * lax.rsqrt(ms + NORM_EPS) * fw_ref[...]


def _out_proj(y_mla, y_ret, proj, x2d, wbm, wbr, wo, fw, *, tm):
    rows = x2d.shape[0]
    width = HEADS * HEAD_V
    const = lambda shape: pl.BlockSpec(shape, lambda i: (0, 0), pipeline_mode=pl.Buffered(1))
    return pl.pallas_call(
        _out_proj_kernel,
        grid=(rows // tm,),
        in_specs=[
            pl.BlockSpec((tm, width), lambda i: (i, 0)),
            pl.BlockSpec((tm, width), lambda i: (i, 0)),
            pl.BlockSpec((tm, D_MODEL), lambda i: (i, 0)),
            pl.BlockSpec((tm, D_MODEL), lambda i: (i, 1)),
            pl.BlockSpec((tm, D_MODEL), lambda i: (i, 0)),
            const((width, D_MODEL)), const((width, D_MODEL)), const((D_MODEL, D_MODEL)),
            const((1, D_MODEL)),
        ],
        out_specs=pl.BlockSpec((tm, D_MODEL), lambda i: (i, 0)),
        out_shape=jax.ShapeDtypeStruct((rows, D_MODEL), _F32),
        compiler_params=pltpu.CompilerParams(
            dimension_semantics=("parallel",), vmem_limit_bytes=VMEM_LIMIT),
        name="out_proj",
    )(y_mla, y_ret, proj, proj, x2d, wbm, wbr, wo, fw)


def _rope_tables(seq):
    pos = jnp.arange(CHUNK + seq, dtype=_F32) - META_PAD
    z32 = jnp.zeros((pos.shape[0], 32), _F32)
    inv64 = ROPE_BASE ** (-jnp.arange(0, ROPE, 2, dtype=_F32) / ROPE)
    ang = pos[:, None] * inv64[None, :]
    cos, sin = jnp.cos(ang), jnp.sin(ang)
    c64 = jnp.concatenate([cos, cos, z32, z32], axis=1)
    s1 = jnp.concatenate([-sin, z32, z32, z32], axis=1)
    s2 = jnp.concatenate([z32, sin, z32, z32], axis=1)
    inv128 = ROPE_BASE ** (-jnp.arange(0, RET_D, 2, dtype=_F32) / RET_D)
    ang = pos[:, None] * inv128[None, :]
    cos, sin = jnp.cos(ang), jnp.sin(ang)
    c128 = jnp.concatenate([cos, cos], axis=1)
    s128 = jnp.concatenate([-sin, sin], axis=1)
    return c64, s1, s2, c128, s128


def _decay_tables():
    log_g = jnp.log1p(-(2.0 ** (-5.0 - jnp.arange(HEADS, dtype=_F32))))
    n = jnp.arange(CHUNK, dtype=_F32)
    diff = n[:, None] - n[None, :]
    dec = jnp.where(diff >= 0, jnp.exp(log_g[:, None, None] * jnp.maximum(diff, 0.0)), 0.0)
    ones = jnp.ones((1, 1, RET_D), _F32)
    zeta = jnp.exp(log_g[:, None] * (CHUNK - 1.0 - n))[:, :, None] * ones
    xi = jnp.exp(log_g[:, None] * (n + 1.0))[:, :, None] * ones
    gch = jnp.exp(log_g * CHUNK)[:, None, None] * jnp.ones((1, CHUNK, RET_D), _F32)
    return dec, zeta, xi, gch


def _pack_w_in(w_in):
    c_q, c_kv, k_pe, z_mla, r_q, r_k, r_v, z_ret, gates = jnp.split(
        w_in.astype(_BF), np.cumsum([512, 256, 64, 1024, 1024, 1024, 1024, 1024]).tolist(), axis=1)
    pad = jnp.zeros((D_MODEL, IN_PAD - KPE_OFF - ROPE), _BF)
    return jnp.concatenate([gates, z_mla, z_ret, r_q, r_k, r_v, c_q, c_kv, k_pe, pad], axis=1)


def _pack_w_uq(w_uq):
    w = w_uq.reshape(Q_RANK, HEADS, NOPE + ROPE)
    w = jnp.pad(w, ((0, 0), (0, 0), (0, QK_PAD - NOPE - ROPE)))
    return w.reshape(Q_RANK, HEADS * QK_PAD).astype(_BF)


def _split_w_ukv(w_ukv):
    w = w_ukv.reshape(KV_RANK, HEADS, NOPE + HEAD_V)
    wk = w[:, :, :NOPE].reshape(KV_RANK, HEADS * NOPE)
    wvt = w[:, :, NOPE:].reshape(KV_RANK, HEADS * HEAD_V).T
    return wk.astype(_BF), wvt.astype(_BF)


def kernel(x, meta, norm_w, w_in, mla_q_norm_w, mla_w_uq, mla_kv_norm_w, mla_w_ukv, ret_gn_w,
           ret_gn_b, w_branch_mla, w_branch_ret, w_out, final_norm_w):
    batch, seq, _ = x.shape
    rows = batch * seq
    x2d = x.reshape(rows, D_MODEL)
    meta_chunk = jnp.pad(meta.astype(x.dtype), ((META_PAD, 0), (0, 0)))

    w_in_bf = _pack_w_in(w_in[0])
    wq = _pack_w_uq(mla_w_uq[0])
    wk, wvt = _split_w_ukv(mla_w_ukv[0])
    qnw, kvnw = mla_q_norm_w, mla_kv_norm_w
    scale = float((NOPE + ROPE) ** -0.5 * np.log2(np.e))

    c64, s1, s2, c128, s128 = _rope_tables(seq)
    tabs64_x = (c64[CHUNK:], s1[CHUNK:], s2[CHUNK:])
    tabs64_m = (c64[:CHUNK], s1[:CHUNK], s2[:CHUNK])
    tabs128_x = (c128[CHUNK:], s128[CHUNK:])
    tabs128_m = (c128[:CHUNK], s128[:CHUNK])

    proj = _in_proj(x2d, norm_w, w_in_bf, tm=1024, tn=1024)
    proj_m = _in_proj(meta_chunk, norm_w, w_in_bf, tm=CHUNK, tn=1024)

    q, k, vt = _mla_prep(proj, qnw, kvnw, wq, wk, wvt, tabs64_x, tm=512, scale=scale)
    _, k_m, vt_m = _mla_prep(proj_m, qnw, kvnw, wq, wk, wvt, tabs64_m, tm=CHUNK, scale=scale)

    proj3 = proj.reshape(batch, seq, IN_PAD)
    y_mla = _attention(q.reshape(batch, seq, -1), k.reshape(batch, seq, -1), vt,
                       k_m, vt_m[0], proj3, tq=512)
    y_ret = _retention(proj3, proj_m, tabs128_x, tabs128_m, _decay_tables(),
                       ret_gn_w, ret_gn_b, group=8)

    out = _out_proj(y_mla.reshape(rows, -1), y_ret.reshape(rows, -1), proj, x2d,
                    w_branch_mla[0].astype(_BF), w_branch_ret[0].astype(_BF),
                    w_out[0].astype(_BF), final_norm_w.reshape(1, D_MODEL), tm=256)
    return out.reshape(batch, seq, D_MODEL)
```

```python
import functools

import numpy as np
import jax
import jax.numpy as jnp
from jax import lax
from jax.experimental import pallas as pl
from jax.experimental.pallas import tpu as pltpu

D_MODEL = 2048
N_META = 16
CHUNK = 128
META_PAD = CHUNK - N_META
HEADS = 8
NOPE = 128
ROPE = 64
HEAD_V = 128
Q_RANK = 512
KV_RANK = 256
QK_PAD = 256
RET_D = 128
ROPE_BASE = 10000.0
NORM_EPS = 1e-6
GN_EPS = 1e-5
NEG_INF = -1e30

CQ_OFF = 0
CKV_OFF = 512
KPE_OFF = 768
ZM_OFF = 1024
RQ_OFF = 2048
RK_OFF = 3072
RV_OFF = 4096
ZR_OFF = 5120
GATE_OFF = 6144
IN_PAD = 10240
KPE_END = KPE_OFF + ROPE

VMEM_LIMIT = 56 * 1024 * 1024

_BF = jnp.bfloat16
_F32 = jnp.float32


def _dot(a, b):
    return jnp.dot(a, b, preferred_element_type=_F32)


def _dot_nt(a, b):
    return lax.dot_general(a, b, (((1,), (1,)), ((), ())), preferred_element_type=_F32)


def _dot_tn(a, b):
    return lax.dot_general(a, b, (((0,), (0,)), ((), ())), preferred_element_type=_F32)


def _in_proj_kernel(x_ref, nw_ref, w_ref, o_ref, xn_ref, *, tn):
    j = pl.program_id(1)

    @pl.when(j == 0)
    def _():
        x = x_ref[...]
        ms = jnp.mean(x * x, axis=-1, keepdims=True)
        xn_ref[...] = (x * lax.rsqrt(ms + NORM_EPS) * nw_ref[...]).astype(_BF)

    acc = _dot(xn_ref[...], w_ref[...])
    col = j * tn
    is_gate = col >= GATE_OFF
    is_z = ((col >= ZM_OFF) & (col < RQ_OFF)) | ((col >= ZR_OFF) & (col < GATE_OFF))

    @pl.when(is_gate)
    def _():
        o_ref[...] = jax.nn.sigmoid(acc).astype(_BF)

    @pl.when(is_z)
    def _():
        o_ref[...] = (acc * jax.nn.sigmoid(acc)).astype(_BF)

    @pl.when(jnp.logical_not(is_gate | is_z))
    def _():
        o_ref[...] = acc.astype(_BF)


def _in_proj(x2d, norm_w, w_in_bf, *, tm, tn):
    rows = x2d.shape[0]
    assert all(off % tn == 0 for off in (ZM_OFF, RQ_OFF, ZR_OFF, GATE_OFF, IN_PAD))
    return pl.pallas_call(
        functools.partial(_in_proj_kernel, tn=tn),
        grid=(rows // tm, IN_PAD // tn),
        in_specs=[
            pl.BlockSpec((tm, D_MODEL), lambda i, j: (i, 0)),
            pl.BlockSpec((1, D_MODEL), lambda i, j: (0, 0)),
            pl.BlockSpec((D_MODEL, tn), lambda i, j: (0, j)),
        ],
        out_specs=pl.BlockSpec((tm, tn), lambda i, j: (i, j)),
        out_shape=jax.ShapeDtypeStruct((rows, IN_PAD), _BF),
        scratch_shapes=[pltpu.VMEM((tm, D_MODEL), _BF)],
        compiler_params=pltpu.CompilerParams(
            dimension_semantics=("parallel", "arbitrary"), vmem_limit_bytes=VMEM_LIMIT),
        name="in_proj",
    )(x2d, norm_w, w_in_bf)


def _rms(x, w):
    ms = jnp.mean(x * x, axis=-1, keepdims=True)
    return x * lax.rsqrt(ms + NORM_EPS) * w


def _rope64(t, c, s1, s2):
    return t * c + pltpu.roll(t, 96, 1) * s1 + pltpu.roll(t, 32, 1) * s2


def _mla_prep_kernel(cq_ref, ckv_ref, kpe_ref, qnw_ref, kvnw_ref, wq_ref, wk_ref, wvt_ref,
                     c_ref, s1_ref, s2_ref, q_ref, k_ref, vt_ref, *, scale):
    c, s1, s2 = c_ref[...], s1_ref[...], s2_ref[...]

    cqn = _rms(cq_ref[...].astype(_F32), qnw_ref[...]).astype(_BF)
    q = _dot(cqn, wq_ref[...])
    for h in range(HEADS):
        lo = h * QK_PAD
        q_ref[:, lo:lo + NOPE] = (q[:, lo:lo + NOPE] * scale).astype(_BF)
        q_ref[:, lo + NOPE:lo + QK_PAD] = (
            _rope64(q[:, lo + NOPE:lo + QK_PAD], c, s1, s2) * scale).astype(_BF)

    ckvn = _rms(ckv_ref[...].astype(_F32), kvnw_ref[...]).astype(_BF)
    kn = _dot(ckvn, wk_ref[...])
    kpe = _rope64(kpe_ref[...].astype(_F32), c, s1, s2).astype(_BF)
    for h in range(HEADS):
        lo = h * QK_PAD
        k_ref[:, lo:lo + NOPE] = kn[:, h * NOPE:(h + 1) * NOPE].astype(_BF)
        k_ref[:, lo + NOPE:lo + QK_PAD] = kpe
    vt_ref[0] = _dot_nt(wvt_ref[...], ckvn).astype(_BF)


def _mla_prep(proj, qnw, kvnw, wq, wk, wvt, tabs, *, tm, scale):
    rows = proj.shape[0]
    seq = tabs[0].shape[0]
    tab_blocks = seq // tm
    full = lambda shape: pl.BlockSpec(shape, lambda i: (0, 0))
    tab_spec = pl.BlockSpec((tm, 128), lambda i: (i % tab_blocks, 0))
    return pl.pallas_call(
        functools.partial(_mla_prep_kernel, scale=scale),
        grid=(rows // tm,),
        in_specs=[
            pl.BlockSpec((tm, Q_RANK), lambda i: (i, CQ_OFF // Q_RANK)),
            pl.BlockSpec((tm, KV_RANK), lambda i: (i, CKV_OFF // KV_RANK)),
            pl.BlockSpec((tm, 128), lambda i: (i, KPE_OFF // 128)),
            full((1, Q_RANK)), full((1, KV_RANK)),
            full((Q_RANK, HEADS * QK_PAD)), full((KV_RANK, HEADS * NOPE)),
            full((HEADS * HEAD_V, KV_RANK)),
            tab_spec, tab_spec, tab_spec,
        ],
        out_specs=[
            pl.BlockSpec((tm, HEADS * QK_PAD), lambda i: (i, 0)),
            pl.BlockSpec((tm, HEADS * QK_PAD), lambda i: (i, 0)),
            pl.BlockSpec((1, HEADS * HEAD_V, tm), lambda i: (i // tab_blocks, 0, i % tab_blocks)),
        ],
        out_shape=[
            jax.ShapeDtypeStruct((rows, HEADS * QK_PAD), _BF),
            jax.ShapeDtypeStruct((rows, HEADS * QK_PAD), _BF),
            jax.ShapeDtypeStruct((rows // seq, HEADS * HEAD_V, seq), _BF),
        ],
        compiler_params=pltpu.CompilerParams(
            dimension_semantics=("parallel",), vmem_limit_bytes=VMEM_LIMIT),
        name="mla_prep",
    )(proj, proj, proj, qnw, kvnw, wq, wk, wvt, *tabs)


def _attn_kernel(q_ref, k_ref, vt_ref, km_ref, vmt_ref, zs_ref, o_ref, m_ref, l_ref, acc_ref,
                 sa_ref, sb_ref, ma_ref, mb_ref, *, seq, tq, tk):
    def update(s, smax, vt, c0):
        m_old = m_ref[:, c0:]
        m_new = jnp.maximum(m_old, smax)
        alpha = jnp.exp2(m_old - m_new)
        p = jnp.exp2(s - m_new)
        l_ref[:, c0:] = alpha * l_ref[:, c0:] + jnp.sum(p, axis=0, keepdims=True)
        acc_ref[:, c0:] = alpha * acc_ref[:, c0:] + _dot(vt, p.astype(_BF))
        m_ref[:, c0:] = m_new

    meta_mask = lax.broadcasted_iota(jnp.int32, (CHUNK, tq), 0) >= META_PAD
    causal = (lax.broadcasted_iota(jnp.int32, (tk, tq), 0)
              <= lax.broadcasted_iota(jnp.int32, (tk, tq), 1))
    buf_a, buf_b = (sa_ref, ma_ref), (sb_ref, mb_ref)

    for qi in range(seq // tq):
        rows = slice(qi * tq, (qi + 1) * tq)
        nfull = qi * tq // tk
        nblk = nfull + tq // tk
        m_ref[...] = jnp.full((1, tq), NEG_INF, _F32)
        l_ref[...] = jnp.zeros((1, tq), _F32)
        acc_ref[...] = jnp.zeros((HEAD_V, tq), _F32)

        def diagonal(blk):
            return isinstance(blk, int) and blk >= nfull

        def col0(blk):
            return (blk - nfull) * tk if diagonal(blk) else 0

        def keys(blk):
            if isinstance(blk, int):
                return slice(blk * tk, (blk + 1) * tk)
            return pl.ds(pl.multiple_of(blk * tk, tk), tk)

        def produce(buf, blk):
            s_ref, smax_ref = buf
            c0 = col0(blk)
            s = _dot_nt(k_ref[0, keys(blk), :], q_ref[0, qi * tq + c0:(qi + 1) * tq, :])
            if diagonal(blk):
                s = jnp.where(causal[:, :tq - c0], s, NEG_INF)
            s_ref[:, c0:] = s
            smax_ref[:, c0:] = jnp.max(s, axis=0, keepdims=True)

        def consume(buf, blk):
            s_ref, smax_ref = buf
            c0 = col0(blk)
            update(s_ref[:, c0:], smax_ref[:, c0:], vt_ref[0, :, keys(blk)], c0)

        produce(buf_a, 0)
        s_meta = jnp.where(meta_mask, _dot_nt(km_ref[...], q_ref[0, rows, :]), NEG_INF)
        update(s_meta, jnp.max(s_meta, axis=0, keepdims=True), vmt_ref[...], 0)

        def pair(i, carry):
            produce(buf_b, 2 * i + 1)
            consume(buf_a, 2 * i)
            produce(buf_a, 2 * i + 2)
            consume(buf_b, 2 * i + 1)
            return carry

        trips = (nfull - 1) // 2 if nfull >= 3 else 0
        if trips:
            lax.fori_loop(0, trips, pair, 0)
        done = 2 * trips
        for blk in range(done, nblk - 1):
            produce(buf_b if blk % 2 == 0 else buf_a, blk + 1)
            consume(buf_a if blk % 2 == 0 else buf_b, blk)
        consume(buf_a if (nblk - 1) % 2 == 0 else buf_b, nblk - 1)

        y = acc_ref[...] * (1.0 / l_ref[...])
        o_ref[0, rows, :] = (y.T * zs_ref[0, rows, :].astype(_F32)).astype(_BF)


def _attention(q, k, vt, k_meta, vt_meta, proj3, *, tq, tk):
    batch, seq = q.shape[0], q.shape[1]
    return pl.pallas_call(
        functools.partial(_attn_kernel, seq=seq, tq=tq, tk=tk),
        grid=(batch, HEADS),
        in_specs=[
            pl.BlockSpec((1, seq, QK_PAD), lambda b, h: (b, 0, h)),
            pl.BlockSpec((1, seq, QK_PAD), lambda b, h: (b, 0, h)),
            pl.BlockSpec((1, HEAD_V, seq), lambda b, h: (b, h, 0)),
            pl.BlockSpec((CHUNK, QK_PAD), lambda b, h: (0, h)),
            pl.BlockSpec((HEAD_V, CHUNK), lambda b, h: (h, 0)),
            pl.BlockSpec((1, seq, HEAD_V), lambda b, h: (b, 0, ZM_OFF // HEAD_V + h)),
        ],
        out_specs=pl.BlockSpec((1, seq, HEAD_V), lambda b, h: (b, 0, h)),
        out_shape=jax.ShapeDtypeStruct((batch, seq, HEADS * HEAD_V), _BF),
        scratch_shapes=[pltpu.VMEM((1, tq), _F32), pltpu.VMEM((1, tq), _F32),
                        pltpu.VMEM((HEAD_V, tq), _F32),
                        pltpu.VMEM((tk, tq), _F32), pltpu.VMEM((tk, tq), _F32),
                        pltpu.VMEM((1, tq), _F32), pltpu.VMEM((1, tq), _F32)],
        compiler_params=pltpu.CompilerParams(
            dimension_semantics=("parallel", "parallel"), vmem_limit_bytes=VMEM_LIMIT),
        name="attention",
    )(q, k, vt, k_meta, vt_meta, proj3)


def _rope128(t, c, s):
    return t * c + pltpu.roll(t, 64, 1) * s


def _ret_kernel(rq_ref, rk_ref, rv_ref, zs_ref, rkm_ref, rvm_ref, c_ref, s_ref, cm_ref, sm_ref,
                dec_ref, zeta_ref, xi_ref, gch_ref, gw_ref, gb_ref, o_ref, *, seq, group):
    kscale = RET_D ** -0.5
    dec = dec_ref[0]
    zeta = zeta_ref[0]
    xi = xi_ref[0]
    gch = gch_ref[0]
    gw = gw_ref[...]
    gb = gb_ref[...]

    km = _rope128(rkm_ref[...].astype(_F32), cm_ref[...], sm_ref[...]) * kscale
    state0 = _dot_tn((km * zeta).astype(_BF), rvm_ref[...])

    rows = group * CHUNK
    bdims = ((0,), (0,))

    def body(gi, state):
        rs = pl.multiple_of(gi * rows, rows)
        c = c_ref[pl.ds(rs, rows), :]
        s = s_ref[pl.ds(rs, rows), :]
        q = _rope128(rq_ref[0, pl.ds(rs, rows), :].astype(_F32), c, s)
        k = _rope128(rk_ref[0, pl.ds(rs, rows), :].astype(_F32), c, s) * kscale
        q3 = q.reshape(group, CHUNK, RET_D)
        k3 = k.reshape(group, CHUNK, RET_D)
        v3 = rv_ref[0, pl.ds(rs, rows), :].reshape(group, CHUNK, RET_D)
        kz = jnp.swapaxes((k3 * zeta[None]).astype(_BF), 1, 2)
        kv = lax.dot_general(kz, v3, (((2,), (1,)), bdims), preferred_element_type=_F32)
        qb = q3.astype(_BF)
        sc = lax.dot_general(qb, k3.astype(_BF), (((2,), (2,)), bdims),
                             preferred_element_type=_F32) * dec[None]
        states = []
        for ci in range(group):
            states.append(state.astype(_BF))
            state = state * gch + kv[ci]
        o = (lax.dot_general(sc.astype(_BF), v3, (((2,), (1,)), bdims),
                             preferred_element_type=_F32)
             + lax.dot_general(qb, jnp.stack(states), (((2,), (1,)), bdims),
                               preferred_element_type=_F32) * xi[None])
        o = o.reshape(rows, RET_D)
        mu = jnp.mean(o, axis=-1, keepdims=True)
        d = o - mu
        var = jnp.mean(d * d, axis=-1, keepdims=True)
        y = d * lax.rsqrt(var + GN_EPS) * gw + gb
        o_ref[0, pl.ds(rs, rows), :] = (
            y * zs_ref[0, pl.ds(rs, rows), :].astype(_F32)).astype(_BF)
        return state

    lax.fori_loop(0, seq // rows, body, state0)


def _retention(proj3, proj_meta, tabs_x, tabs_m, consts, gn_w, gn_b, *, group):
    batch, seq = proj3.shape[0], proj3.shape[1]
    col = lambda off: (lambda b, h: (b, 0, off // RET_D + h))
    mcol = lambda off: (lambda b, h: (0, off // RET_D + h))
    head3 = pl.BlockSpec((1, CHUNK, RET_D), lambda b, h: (h, 0, 0))
    return pl.pallas_call(
        functools.partial(_ret_kernel, seq=seq, group=group),
        grid=(batch, HEADS),
        in_specs=[
            pl.BlockSpec((1, seq, RET_D), col(RQ_OFF)),
            pl.BlockSpec((1, seq, RET_D), col(RK_OFF)),
            pl.BlockSpec((1, seq, RET_D), col(RV_OFF)),
            pl.BlockSpec((1, seq, RET_D), col(ZR_OFF)),
            pl.BlockSpec((CHUNK, RET_D), mcol(RK_OFF)),
            pl.BlockSpec((CHUNK, RET_D), mcol(RV_OFF)),
            pl.BlockSpec((seq, RET_D), lambda b, h: (0, 0)),
            pl.BlockSpec((seq, RET_D), lambda b, h: (0, 0)),
            pl.BlockSpec((CHUNK, RET_D), lambda b, h: (0, 0)),
            pl.BlockSpec((CHUNK, RET_D), lambda b, h: (0, 0)),
            head3, head3, head3, head3,
            pl.BlockSpec((1, RET_D), lambda b, h: (0, h)),
            pl.BlockSpec((1, RET_D), lambda b, h: (0, h)),
        ],
        out_specs=pl.BlockSpec((1, seq, RET_D), lambda b, h: (b, 0, h)),
        out_shape=jax.ShapeDtypeStruct((batch, seq, HEADS * RET_D), _BF),
        compiler_params=pltpu.CompilerParams(
            dimension_semantics=("parallel", "parallel"), vmem_limit_bytes=VMEM_LIMIT),
        name="retention",
    )(proj3, proj3, proj3, proj3, proj_meta, proj_meta, *tabs_x, *tabs_m, *consts, gn_w, gn_b)


def _out_proj_kernel(ym_ref, yr_ref, g0_ref, g1_ref, x_ref, wbm_ref, wbr_ref, wo_ref, fw_ref, o_ref):
    m1 = _dot(ym_ref[...], wbm_ref[...])
    m2 = _dot(yr_ref[...], wbr_ref[...])
    merged = g0_ref[...].astype(_F32) * m1 + g1_ref[...].astype(_F32) * m2
    h = x_ref[...] + _dot(merged.astype(_BF), wo_ref[...])
    ms = jnp.mean(h * h, axis=-1, keepdims=True)
    o_ref[...] = h * lax.rsqrt(ms + NORM_EPS) * fw_ref[...]


def _out_proj(y_mla, y_ret, proj, x2d, wbm, wbr, wo, fw, *, tm):
    rows = x2d.shape[0]
    width = HEADS * HEAD_V
    const = lambda shape: pl.BlockSpec(shape, lambda i: (0, 0), pipeline_mode=pl.Buffered(1))
    return pl.pallas_call(
        _out_proj_kernel,
        grid=(rows // tm,),
        in_specs=[
            pl.BlockSpec((tm, width), lambda i: (i, 0)),
            pl.BlockSpec((tm, width), lambda i: (i, 0)),
            pl.BlockSpec((tm, D_MODEL), lambda i: (i, GATE_OFF // D_MODEL)),
            pl.BlockSpec((tm, D_MODEL), lambda i: (i, GATE_OFF // D_MODEL + 1)),
            pl.BlockSpec((tm, D_MODEL), lambda i: (i, 0)),
            const((width, D_MODEL)), const((width, D_MODEL)), const((D_MODEL, D_MODEL)),
            const((1, D_MODEL)),
        ],
        out_specs=pl.BlockSpec((tm, D_MODEL), lambda i: (i, 0)),
        out_shape=jax.ShapeDtypeStruct((rows, D_MODEL), _F32),
        compiler_params=pltpu.CompilerParams(
            dimension_semantics=("parallel",), vmem_limit_bytes=VMEM_LIMIT),
        name="out_proj",
    )(y_mla, y_ret, proj, proj, x2d, wbm, wbr, wo, fw)


def _rope_tables(seq):
    pos = jnp.arange(CHUNK + seq, dtype=_F32) - META_PAD
    z32 = jnp.zeros((pos.shape[0], 32), _F32)
    inv64 = ROPE_BASE ** (-jnp.arange(0, ROPE, 2, dtype=_F32) / ROPE)
    ang = pos[:, None] * inv64[None, :]
    cos, sin = jnp.cos(ang), jnp.sin(ang)
    c64 = jnp.concatenate([cos, cos, z32, z32], axis=1)
    s1 = jnp.concatenate([-sin, z32, z32, z32], axis=1)
    s2 = jnp.concatenate([z32, sin, z32, z32], axis=1)
    inv128 = ROPE_BASE ** (-jnp.arange(0, RET_D, 2, dtype=_F32) / RET_D)
    ang = pos[:, None] * inv128[None, :]
    cos, sin = jnp.cos(ang), jnp.sin(ang)
    c128 = jnp.concatenate([cos, cos], axis=1)
    s128 = jnp.concatenate([-sin, sin], axis=1)
    return c64, s1, s2, c128, s128


def _decay_tables():
    log_g = jnp.log1p(-(2.0 ** (-5.0 - jnp.arange(HEADS, dtype=_F32))))
    n = jnp.arange(CHUNK, dtype=_F32)
    diff = n[:, None] - n[None, :]
    dec = jnp.where(diff >= 0, jnp.exp(log_g[:, None, None] * jnp.maximum(diff, 0.0)), 0.0)
    ones = jnp.ones((1, 1, RET_D), _F32)
    zeta = jnp.exp(log_g[:, None] * (CHUNK - 1.0 - n))[:, :, None] * ones
    xi = jnp.exp(log_g[:, None] * (n + 1.0))[:, :, None] * ones
    gch = jnp.exp(log_g * CHUNK)[:, None, None] * jnp.ones((1, CHUNK, RET_D), _F32)
    return dec, zeta, xi, gch


def _pack_w_in(w_in):
    pad = jnp.zeros((D_MODEL, ZM_OFF - KPE_END), _BF)
    return jnp.concatenate(
        [w_in[:, :KPE_END].astype(_BF), pad, w_in[:, KPE_END:].astype(_BF)], axis=1)


def _pack_w_uq(w_uq):
    w = w_uq.reshape(Q_RANK, HEADS, NOPE + ROPE)
    w = jnp.pad(w, ((0, 0), (0, 0), (0, QK_PAD - NOPE - ROPE)))
    return w.reshape(Q_RANK, HEADS * QK_PAD).astype(_BF)


def _split_w_ukv(w_ukv):
    w = w_ukv.reshape(KV_RANK, HEADS, NOPE + HEAD_V)
    wk = w[:, :, :NOPE].reshape(KV_RANK, HEADS * NOPE)
    wvt = w[:, :, NOPE:].reshape(KV_RANK, HEADS * HEAD_V).T
    return wk.astype(_BF), wvt.astype(_BF)


def kernel(x, meta, norm_w, w_in, mla_q_norm_w, mla_w_uq, mla_kv_norm_w, mla_w_ukv, ret_gn_w,
           ret_gn_b, w_branch_mla, w_branch_ret, w_out, final_norm_w):
    batch, seq, _ = x.shape
    rows = batch * seq
    x2d = x.reshape(rows, D_MODEL)
    meta_chunk = jnp.pad(meta.astype(x.dtype), ((META_PAD, 0), (0, 0)))

    w_in_bf = _pack_w_in(w_in[0])
    wq = _pack_w_uq(mla_w_uq[0])
    wk, wvt = _split_w_ukv(mla_w_ukv[0])
    qnw, kvnw = mla_q_norm_w, mla_kv_norm_w
    scale = float((NOPE + ROPE) ** -0.5 * np.log2(np.e))

    c64, s1, s2, c128, s128 = _rope_tables(seq)
    tabs64_x = (c64[CHUNK:], s1[CHUNK:], s2[CHUNK:])
    tabs64_m = (c64[:CHUNK], s1[:CHUNK], s2[:CHUNK])
    tabs128_x = (c128[CHUNK:], s128[CHUNK:])
    tabs128_m = (c128[:CHUNK], s128[:CHUNK])

    proj = _in_proj(x2d, norm_w, w_in_bf, tm=1024, tn=1024)
    proj_m = _in_proj(meta_chunk, norm_w, w_in_bf, tm=CHUNK, tn=1024)

    q, k, vt = _mla_prep(proj, qnw, kvnw, wq, wk, wvt, tabs64_x, tm=512, scale=scale)
    _, k_m, vt_m = _mla_prep(proj_m, qnw, kvnw, wq, wk, wvt, tabs64_m, tm=CHUNK, scale=scale)

    proj3 = proj.reshape(batch, seq, IN_PAD)
    y_mla = _attention(q.reshape(batch, seq, -1), k.reshape(batch, seq, -1), vt,
                       k_m, vt_m[0], proj3, tq=1024, tk=512)
    y_ret = _retention(proj3, proj_m, tabs128_x, tabs128_m, _decay_tables(),
                       ret_gn_w, ret_gn_b, group=8)

    out = _out_proj(y_mla.reshape(rows, -1), y_ret.reshape(rows, -1), proj, x2d,
                    w_branch_mla[0].astype(_BF), w_branch_ret[0].astype(_BF),
                    w_out[0].astype(_BF), final_norm_w.reshape(1, D_MODEL), tm=256)
    return out.reshape(batch, seq, D_MODEL)
```

```python
import functools

import numpy as np
import jax
import jax.numpy as jnp
from jax import lax
from jax.experimental import pallas as pl
from jax.experimental.pallas import tpu as pltpu

D_MODEL = 2048
N_META = 16
CHUNK = 128
META_PAD = CHUNK - N_META
HEADS = 8
NOPE = 128
ROPE = 64
HEAD_V = 128
Q_RANK = 512
KV_RANK = 256
QK_PAD = 256
RET_D = 128
ROPE_BASE = 10000.0
NORM_EPS = 1e-6
GN_EPS = 1e-5
NEG_INF = -1e30

CQ_OFF = 0
CKV_OFF = 512
KPE_OFF = 768
ZM_OFF = 1024
RQ_OFF = 2048
RK_OFF = 3072
RV_OFF = 4096
ZR_OFF = 5120
GATE_OFF = 6144
IN_PAD = 10240
KPE_END = KPE_OFF + ROPE

VMEM_LIMIT = 56 * 1024 * 1024

_BF = jnp.bfloat16
_F32 = jnp.float32


def _dot(a, b):
    return jnp.dot(a, b, preferred_element_type=_F32)


def _dot_nt(a, b):
    return lax.dot_general(a, b, (((1,), (1,)), ((), ())), preferred_element_type=_F32)


def _dot_tn(a, b):
    return lax.dot_general(a, b, (((0,), (0,)), ((), ())), preferred_element_type=_F32)


def _in_proj_kernel(x_ref, nw_ref, w_hbm, o_hbm, xn_ref, wbuf, obuf, wsem, osem, *, tm, tn):
    i = pl.program_id(0)
    steps = pl.num_programs(0)
    tiles = IN_PAD // tn

    def w_copy(j):
        return pltpu.make_async_copy(
            w_hbm.at[:, pl.ds(j * tn, tn)], wbuf.at[j % 2], wsem.at[j % 2])

    def o_copy(j):
        return pltpu.make_async_copy(
            obuf.at[j % 2], o_hbm.at[pl.ds(i * tm, tm), pl.ds(j * tn, tn)], osem.at[j % 2])

    @pl.when(i == 0)
    def _():
        w_copy(0).start()

    x = x_ref[...]
    ms = jnp.mean(x * x, axis=-1, keepdims=True)
    xn_ref[...] = (x * lax.rsqrt(ms + NORM_EPS) * nw_ref[...]).astype(_BF)

    def finish(acc, j):
        col = j * tn
        if col >= GATE_OFF:
            obuf[j % 2] = jax.nn.sigmoid(acc).astype(_BF)
        elif ZM_OFF <= col < RQ_OFF or ZR_OFF <= col < GATE_OFF:
            obuf[j % 2] = (acc * jax.nn.sigmoid(acc)).astype(_BF)
        else:
            obuf[j % 2] = acc.astype(_BF)

    def free_slot_for(j):
        if j >= 2:
            o_copy(j - 2).wait()
        else:
            @pl.when(i > 0)
            def _():
                o_copy(tiles - 2 + j).wait()

    acc_prev = None
    for j in range(tiles):
        w_copy(j).wait()
        if j + 1 < tiles:
            w_copy(j + 1).start()
        else:
            @pl.when(i + 1 < steps)
            def _():
                w_copy(0).start()
        if j >= 2:
            o_copy(j - 2).start()
        if j >= 1:
            free_slot_for(j - 1)
        acc = _dot(xn_ref[...], wbuf[j % 2])
        if j >= 1:
            finish(acc_prev, j - 1)
        acc_prev = acc

    o_copy(tiles - 2).start()
    free_slot_for(tiles - 1)
    finish(acc_prev, tiles - 1)
    o_copy(tiles - 1).start()

    @pl.when(i + 1 == steps)
    def _():
        o_copy(tiles - 2).wait()
        o_copy(tiles - 1).wait()


def _in_proj(x2d, norm_w, w_in_bf, *, tm, tn):
    rows = x2d.shape[0]
    assert all(off % tn == 0 for off in (ZM_OFF, RQ_OFF, ZR_OFF, GATE_OFF, IN_PAD))
    assert (IN_PAD // tn) % 2 == 0
    return pl.pallas_call(
        functools.partial(_in_proj_kernel, tm=tm, tn=tn),
        grid=(rows // tm,),
        in_specs=[
            pl.BlockSpec((tm, D_MODEL), lambda i: (i, 0)),
            pl.BlockSpec((1, D_MODEL), lambda i: (0, 0)),
            pl.BlockSpec(memory_space=pl.ANY),
        ],
        out_specs=pl.BlockSpec(memory_space=pl.ANY),
        out_shape=jax.ShapeDtypeStruct((rows, IN_PAD), _BF),
        scratch_shapes=[
            pltpu.VMEM((tm, D_MODEL), _BF),
            pltpu.VMEM((2, D_MODEL, tn), _BF),
            pltpu.VMEM((2, tm, tn), _BF),
            pltpu.SemaphoreType.DMA((2,)),
            pltpu.SemaphoreType.DMA((2,)),
        ],
        compiler_params=pltpu.CompilerParams(
            dimension_semantics=("arbitrary",), vmem_limit_bytes=VMEM_LIMIT),
        name="in_proj",
    )(x2d, norm_w, w_in_bf)


def _rms(x, w):
    ms = jnp.mean(x * x, axis=-1, keepdims=True)
    return x * lax.rsqrt(ms + NORM_EPS) * w


def _rope64(t, c, s1, s2):
    return t * c + pltpu.roll(t, 96, 1) * s1 + pltpu.roll(t, 32, 1) * s2


def _mla_prep_kernel(cq_ref, ckv_ref, kpe_ref, qnw_ref, kvnw_ref, wq_ref, wk_ref, wvt_ref,
                     c_ref, s1_ref, s2_ref, q_ref, k_ref, vt_ref, *, scale):
    c, s1, s2 = c_ref[...], s1_ref[...], s2_ref[...]

    cqn = _rms(cq_ref[...].astype(_F32), qnw_ref[...]).astype(_BF)
    q = _dot(cqn, wq_ref[...])
    for h in range(HEADS):
        lo = h * QK_PAD
        q_ref[:, lo:lo + NOPE] = (q[:, lo:lo + NOPE] * scale).astype(_BF)
        q_ref[:, lo + NOPE:lo + QK_PAD] = (
            _rope64(q[:, lo + NOPE:lo + QK_PAD], c, s1, s2) * scale).astype(_BF)

    ckvn = _rms(ckv_ref[...].astype(_F32), kvnw_ref[...]).astype(_BF)
    kn = _dot(ckvn, wk_ref[...])
    kpe = _rope64(kpe_ref[...].astype(_F32), c, s1, s2).astype(_BF)
    for h in range(HEADS):
        lo = h * QK_PAD
        k_ref[:, lo:lo + NOPE] = kn[:, h * NOPE:(h + 1) * NOPE].astype(_BF)
        k_ref[:, lo + NOPE:lo + QK_PAD] = kpe
    vt_ref[0] = _dot_nt(wvt_ref[...], ckvn).astype(_BF)


def _mla_prep(proj, qnw, kvnw, wq, wk, wvt, tabs, *, tm, scale):
    rows = proj.shape[0]
    seq = tabs[0].shape[0]
    tab_blocks = seq // tm
    full = lambda shape: pl.BlockSpec(shape, lambda i: (0, 0))
    tab_spec = pl.BlockSpec((tm, 128), lambda i: (i % tab_blocks, 0))
    return pl.pallas_call(
        functools.partial(_mla_prep_kernel, scale=scale),
        grid=(rows // tm,),
        in_specs=[
            pl.BlockSpec((tm, Q_RANK), lambda i: (i, CQ_OFF // Q_RANK)),
            pl.BlockSpec((tm, KV_RANK), lambda i: (i, CKV_OFF // KV_RANK)),
            pl.BlockSpec((tm, 128), lambda i: (i, KPE_OFF // 128)),
            full((1, Q_RANK)), full((1, KV_RANK)),
            full((Q_RANK, HEADS * QK_PAD)), full((KV_RANK, HEADS * NOPE)),
            full((HEADS * HEAD_V, KV_RANK)),
            tab_spec, tab_spec, tab_spec,
        ],
        out_specs=[
            pl.BlockSpec((tm, HEADS * QK_PAD), lambda i: (i, 0)),
            pl.BlockSpec((tm, HEADS * QK_PAD), lambda i: (i, 0)),
            pl.BlockSpec((1, HEADS * HEAD_V, tm), lambda i: (i // tab_blocks, 0, i % tab_blocks)),
        ],
        out_shape=[
            jax.ShapeDtypeStruct((rows, HEADS * QK_PAD), _BF),
            jax.ShapeDtypeStruct((rows, HEADS * QK_PAD), _BF),
            jax.ShapeDtypeStruct((rows // seq, HEADS * HEAD_V, seq), _BF),
        ],
        compiler_params=pltpu.CompilerParams(
            dimension_semantics=("parallel",), vmem_limit_bytes=VMEM_LIMIT),
        name="mla_prep",
    )(proj, proj, proj, qnw, kvnw, wq, wk, wvt, *tabs)


def _attn_kernel(q_ref, k_ref, vt_ref, km_ref, vmt_ref, zs_ref, o_ref, m_ref, l_ref, acc_ref,
                 sa_ref, sb_ref, ma_ref, mb_ref, *, seq, tq, tk):
    def update(s, smax, vt, c0):
        m_old = m_ref[:, c0:]
        m_new = jnp.maximum(m_old, smax)
        alpha = jnp.exp2(m_old - m_new)
        p = jnp.exp2(s - m_new)
        l_ref[:, c0:] = alpha * l_ref[:, c0:] + jnp.sum(p, axis=0, keepdims=True)
        acc_ref[:, c0:] = alpha * acc_ref[:, c0:] + _dot(vt, p.astype(_BF))
        m_ref[:, c0:] = m_new

    meta_mask = lax.broadcasted_iota(jnp.int32, (CHUNK, tq), 0) >= META_PAD
    causal = (lax.broadcasted_iota(jnp.int32, (tk, tq), 0)
              <= lax.broadcasted_iota(jnp.int32, (tk, tq), 1))
    buf_a, buf_b = (sa_ref, ma_ref), (sb_ref, mb_ref)

    for qi in range(seq // tq):
        rows = slice(qi * tq, (qi + 1) * tq)
        nfull = qi * tq // tk
        nblk = nfull + tq // tk
        m_ref[...] = jnp.full((1, tq), NEG_INF, _F32)
        l_ref[...] = jnp.zeros((1, tq), _F32)
        acc_ref[...] = jnp.zeros((HEAD_V, tq), _F32)

        def diagonal(blk):
            return isinstance(blk, int) and blk >= nfull

        def col0(blk):
            return (blk - nfull) * tk if diagonal(blk) else 0

        def keys(blk):
            if isinstance(blk, int):
                return slice(blk * tk, (blk + 1) * tk)
            return pl.ds(pl.multiple_of(blk * tk, tk), tk)

        def produce(buf, blk):
            s_ref, smax_ref = buf
            c0 = col0(blk)
            s = _dot_nt(k_ref[0, keys(blk), :], q_ref[0, qi * tq + c0:(qi + 1) * tq, :])
            if diagonal(blk):
                s = jnp.where(causal[:, :tq - c0], s, NEG_INF)
            s_ref[:, c0:] = s
            smax_ref[:, c0:] = jnp.max(s, axis=0, keepdims=True)

        def consume(buf, blk):
            s_ref, smax_ref = buf
            c0 = col0(blk)
            update(s_ref[:, c0:], smax_ref[:, c0:], vt_ref[0, :, keys(blk)], c0)

        produce(buf_a, 0)
        s_meta = jnp.where(meta_mask, _dot_nt(km_ref[...], q_ref[0, rows, :]), NEG_INF)
        update(s_meta, jnp.max(s_meta, axis=0, keepdims=True), vmt_ref[...], 0)

        def pair(i, carry):
            produce(buf_b, 2 * i + 1)
            consume(buf_a, 2 * i)
            produce(buf_a, 2 * i + 2)
            consume(buf_b, 2 * i + 1)
            return carry

        trips = (nfull - 1) // 2 if nfull >= 3 else 0
        if trips:
            lax.fori_loop(0, trips, pair, 0)
        done = 2 * trips
        for blk in range(done, nblk - 1):
            produce(buf_b if blk % 2 == 0 else buf_a, blk + 1)
            consume(buf_a if blk % 2 == 0 else buf_b, blk)
        consume(buf_a if (nblk - 1) % 2 == 0 else buf_b, nblk - 1)

        y = acc_ref[...] * (1.0 / l_ref[...])
        o_ref[0, rows, :] = (y.T * zs_ref[0, rows, :].astype(_F32)).astype(_BF)


def _attention(q, k, vt, k_meta, vt_meta, proj3, *, tq, tk):
    batch, seq = q.shape[0], q.shape[1]
    return pl.pallas_call(
        functools.partial(_attn_kernel, seq=seq, tq=tq, tk=tk),
        grid=(batch, HEADS),
        in_specs=[
            pl.BlockSpec((1, seq, QK_PAD), lambda b, h: (b, 0, h)),
            pl.BlockSpec((1, seq, QK_PAD), lambda b, h: (b, 0, h)),
            pl.BlockSpec((1, HEAD_V, seq), lambda b, h: (b, h, 0)),
            pl.BlockSpec((CHUNK, QK_PAD), lambda b, h: (0, h)),
            pl.BlockSpec((HEAD_V, CHUNK), lambda b, h: (h, 0)),
            pl.BlockSpec((1, seq, HEAD_V), lambda b, h: (b, 0, ZM_OFF // HEAD_V + h)),
        ],
        out_specs=pl.BlockSpec((1, seq, HEAD_V), lambda b, h: (b, 0, h)),
        out_shape=jax.ShapeDtypeStruct((batch, seq, HEADS * HEAD_V), _BF),
        scratch_shapes=[pltpu.VMEM((1, tq), _F32), pltpu.VMEM((1, tq), _F32),
                        pltpu.VMEM((HEAD_V, tq), _F32),
                        pltpu.VMEM((tk, tq), _F32), pltpu.VMEM((tk, tq), _F32),
                        pltpu.VMEM((1, tq), _F32), pltpu.VMEM((1, tq), _F32)],
        compiler_params=pltpu.CompilerParams(
            dimension_semantics=("parallel", "parallel"), vmem_limit_bytes=VMEM_LIMIT),
        name="attention",
    )(q, k, vt, k_meta, vt_meta, proj3)


def _rope128(t, c, s):
    return t * c + pltpu.roll(t, 64, 1) * s


def _ret_kernel(rq_ref, rk_ref, rv_ref, zs_ref, rkm_ref, rvm_ref, c_ref, s_ref, cm_ref, sm_ref,
                dec_ref, zeta_ref, xi_ref, gch_ref, gw_ref, gb_ref, o_ref, *, seq, group):
    kscale = RET_D ** -0.5
    dec = dec_ref[0]
    zeta = zeta_ref[0]
    xi = xi_ref[0]
    gch = gch_ref[0]
    gw = gw_ref[...]
    gb = gb_ref[...]

    km = _rope128(rkm_ref[...].astype(_F32), cm_ref[...], sm_ref[...]) * kscale
    state0 = _dot_tn((km * zeta).astype(_BF), rvm_ref[...])

    rows = group * CHUNK
    bdims = ((0,), (0,))

    def body(gi, state):
        rs = pl.multiple_of(gi * rows, rows)
        c = c_ref[pl.ds(rs, rows), :]
        s = s_ref[pl.ds(rs, rows), :]
        q = _rope128(rq_ref[0, pl.ds(rs, rows), :].astype(_F32), c, s)
        k = _rope128(rk_ref[0, pl.ds(rs, rows), :].astype(_F32), c, s) * kscale
        q3 = q.reshape(group, CHUNK, RET_D)
        k3 = k.reshape(group, CHUNK, RET_D)
        v3 = rv_ref[0, pl.ds(rs, rows), :].reshape(group, CHUNK, RET_D)
        kz = jnp.swapaxes((k3 * zeta[None]).astype(_BF), 1, 2)
        kv = lax.dot_general(kz, v3, (((2,), (1,)), bdims), preferred_element_type=_F32)
        qb = q3.astype(_BF)
        sc = lax.dot_general(qb, k3.astype(_BF), (((2,), (2,)), bdims),
                             preferred_element_type=_F32) * dec[None]
        states = []
        for ci in range(group):
            states.append(state.astype(_BF))
            state = state * gch + kv[ci]
        o = (lax.dot_general(sc.astype(_BF), v3, (((2,), (1,)), bdims),
                             preferred_element_type=_F32)
             + lax.dot_general(qb, jnp.stack(states), (((2,), (1,)), bdims),
                               preferred_element_type=_F32) * xi[None])
        o = o.reshape(rows, RET_D)
        mu = jnp.mean(o, axis=-1, keepdims=True)
        d = o - mu
        var = jnp.mean(d * d, axis=-1, keepdims=True)
        y = d * lax.rsqrt(var + GN_EPS) * gw + gb
        o_ref[0, pl.ds(rs, rows), :] = (
            y * zs_ref[0, pl.ds(rs, rows), :].astype(_F32)).astype(_BF)
        return state

    lax.fori_loop(0, seq // rows, body, state0)


def _retention(proj3, proj_meta, tabs_x, tabs_m, consts, gn_w, gn_b, *, group):
    batch, seq = proj3.shape[0], proj3.shape[1]
    col = lambda off: (lambda b, h: (b, 0, off // RET_D + h))
    mcol = lambda off: (lambda b, h: (0, off // RET_D + h))
    head3 = pl.BlockSpec((1, CHUNK, RET_D), lambda b, h: (h, 0, 0))
    return pl.pallas_call(
        functools.partial(_ret_kernel, seq=seq, group=group),
        grid=(batch, HEADS),
        in_specs=[
            pl.BlockSpec((1, seq, RET_D), col(RQ_OFF)),
            pl.BlockSpec((1, seq, RET_D), col(RK_OFF)),
            pl.BlockSpec((1, seq, RET_D), col(RV_OFF)),
            pl.BlockSpec((1, seq, RET_D), col(ZR_OFF)),
            pl.BlockSpec((CHUNK, RET_D), mcol(RK_OFF)),
            pl.BlockSpec((CHUNK, RET_D), mcol(RV_OFF)),
            pl.BlockSpec((seq, RET_D), lambda b, h: (0, 0)),
            pl.BlockSpec((seq, RET_D), lambda b, h: (0, 0)),
            pl.BlockSpec((CHUNK, RET_D), lambda b, h: (0, 0)),
            pl.BlockSpec((CHUNK, RET_D), lambda b, h: (0, 0)),
            head3, head3, head3, head3,
            pl.BlockSpec((1, RET_D), lambda b, h: (0, h)),
            pl.BlockSpec((1, RET_D), lambda b, h: (0, h)),
        ],
        out_specs=pl.BlockSpec((1, seq, RET_D), lambda b, h: (b, 0, h)),
        out_shape=jax.ShapeDtypeStruct((batch, seq, HEADS * RET_D), _BF),
        compiler_params=pltpu.CompilerParams(
            dimension_semantics=("parallel", "parallel"), vmem_limit_bytes=VMEM_LIMIT),
        name="retention",
    )(proj3, proj3, proj3, proj3, proj_meta, proj_meta, *tabs_x, *tabs_m, *consts, gn_w, gn_b)


def _out_proj_kernel(ym_ref, yr_ref, g0_ref, g1_ref, x_ref, wbm_ref, wbr_ref, wo_ref, fw_ref, o_ref):
    m1 = _dot(ym_ref[...], wbm_ref[...])
    m2 = _dot(yr_ref[...], wbr_ref[...])
    merged = g0_ref[...].astype(_F32) * m1 + g1_ref[...].astype(_F32) * m2
    h = x_ref[...] + _dot(merged.astype(_BF), wo_ref[...])
    ms = jnp.mean(h * h, axis=-1, keepdims=True)
    o_ref[...] = h * lax.rsqrt(ms + NORM_EPS) * fw_ref[...]


def _out_proj(y_mla, y_ret, proj, x2d, wbm, wbr, wo, fw, *, tm):
    rows = x2d.shape[0]
    width = HEADS * HEAD_V
    const = lambda shape: pl.BlockSpec(shape, lambda i: (0, 0), pipeline_mode=pl.Buffered(1))
    return pl.pallas_call(
        _out_proj_kernel,
        grid=(rows // tm,),
        in_specs=[
            pl.BlockSpec((tm, width), lambda i: (i, 0)),
            pl.BlockSpec((tm, width), lambda i: (i, 0)),
            pl.BlockSpec((tm, D_MODEL), lambda i: (i, GATE_OFF // D_MODEL)),
            pl.BlockSpec((tm, D_MODEL), lambda i: (i, GATE_OFF // D_MODEL + 1)),
            pl.BlockSpec((tm, D_MODEL), lambda i: (i, 0)),
            const((width, D_MODEL)), const((width, D_MODEL)), const((D_MODEL, D_MODEL)),
            const((1, D_MODEL)),
        ],
        out_specs=pl.BlockSpec((tm, D_MODEL), lambda i: (i, 0)),
        out_shape=jax.ShapeDtypeStruct((rows, D_MODEL), _F32),
        compiler_params=pltpu.CompilerParams(
            dimension_semantics=("parallel",), vmem_limit_bytes=VMEM_LIMIT),
        name="out_proj",
    )(y_mla, y_ret, proj, proj, x2d, wbm, wbr, wo, fw)


def _rope_tables(seq):
    pos = jnp.arange(CHUNK + seq, dtype=_F32) - META_PAD
    z32 = jnp.zeros((pos.shape[0], 32), _F32)
    inv64 = ROPE_BASE ** (-jnp.arange(0, ROPE, 2, dtype=_F32) / ROPE)
    ang = pos[:, None] * inv64[None, :]
    cos, sin = jnp.cos(ang), jnp.sin(ang)
    c64 = jnp.concatenate([cos, cos, z32, z32], axis=1)
    s1 = jnp.concatenate([-sin, z32, z32, z32], axis=1)
    s2 = jnp.concatenate([z32, sin, z32, z32], axis=1)
    inv128 = ROPE_BASE ** (-jnp.arange(0, RET_D, 2, dtype=_F32) / RET_D)
    ang = pos[:, None] * inv128[None, :]
    cos, sin = jnp.cos(ang), jnp.sin(ang)
    c128 = jnp.concatenate([cos, cos], axis=1)
    s128 = jnp.concatenate([-sin, sin], axis=1)
    return c64, s1, s2, c128, s128


def _decay_tables():
    log_g = jnp.log1p(-(2.0 ** (-5.0 - jnp.arange(HEADS, dtype=_F32))))
    n = jnp.arange(CHUNK, dtype=_F32)
    diff = n[:, None] - n[None, :]
    dec = jnp.where(diff >= 0, jnp.exp(log_g[:, None, None] * jnp.maximum(diff, 0.0)), 0.0)
    ones = jnp.ones((1, 1, RET_D), _F32)
    zeta = jnp.exp(log_g[:, None] * (CHUNK - 1.0 - n))[:, :, None] * ones
    xi = jnp.exp(log_g[:, None] * (n + 1.0))[:, :, None] * ones
    gch = jnp.exp(log_g * CHUNK)[:, None, None] * jnp.ones((1, CHUNK, RET_D), _F32)
    return dec, zeta, xi, gch


def _pack_w_in_kernel(w_ref, o_ref):
    w = w_ref[...]
    o_ref[:, :KPE_END] = w[:, :KPE_END].astype(_BF)
    o_ref[:, KPE_END:ZM_OFF] = jnp.zeros((w.shape[0], ZM_OFF - KPE_END), _BF)
    o_ref[:, ZM_OFF:] = w[:, KPE_END:].astype(_BF)


def _pack_w_in(w_in, *, tr):
    width = w_in.shape[1]
    return pl.pallas_call(
        _pack_w_in_kernel,
        grid=(D_MODEL // tr,),
        in_specs=[pl.BlockSpec((tr, width), lambda i: (i, 0))],
        out_specs=pl.BlockSpec((tr, IN_PAD), lambda i: (i, 0)),
        out_shape=jax.ShapeDtypeStruct((D_MODEL, IN_PAD), _BF),
        compiler_params=pltpu.CompilerParams(
            dimension_semantics=("parallel",), vmem_limit_bytes=VMEM_LIMIT),
        name="pack_w_in",
    )(w_in)


def _pack_w_uq(w_uq):
    w = w_uq.reshape(Q_RANK, HEADS, NOPE + ROPE)
    w = jnp.pad(w, ((0, 0), (0, 0), (0, QK_PAD - NOPE - ROPE)))
    return w.reshape(Q_RANK, HEADS * QK_PAD).astype(_BF)


def _split_w_ukv(w_ukv):
    w = w_ukv.reshape(KV_RANK, HEADS, NOPE + HEAD_V)
    wk = w[:, :, :NOPE].reshape(KV_RANK, HEADS * NOPE)
    wvt = w[:, :, NOPE:].reshape(KV_RANK, HEADS * HEAD_V).T
    return wk.astype(_BF), wvt.astype(_BF)


def kernel(x, meta, norm_w, w_in, mla_q_norm_w, mla_w_uq, mla_kv_norm_w, mla_w_ukv, ret_gn_w,
           ret_gn_b, w_branch_mla, w_branch_ret, w_out, final_norm_w):
    batch, seq, _ = x.shape
    rows = batch * seq
    x2d = x.reshape(rows, D_MODEL)
    meta_chunk = jnp.pad(meta.astype(x.dtype), ((META_PAD, 0), (0, 0)))

    w_in_bf = _pack_w_in(w_in[0], tr=256)
    wq = _pack_w_uq(mla_w_uq[0])
    wk, wvt = _split_w_ukv(mla_w_ukv[0])
    qnw, kvnw = mla_q_norm_w, mla_kv_norm_w
    scale = float((NOPE + ROPE) ** -0.5 * np.log2(np.e))

    c64, s1, s2, c128, s128 = _rope_tables(seq)
    tabs64_x = (c64[CHUNK:], s1[CHUNK:], s2[CHUNK:])
    tabs64_m = (c64[:CHUNK], s1[:CHUNK], s2[:CHUNK])
    tabs128_x = (c128[CHUNK:], s128[CHUNK:])
    tabs128_m = (c128[:CHUNK], s128[:CHUNK])

    proj = _in_proj(x2d, norm_w, w_in_bf, tm=1024, tn=1024)
    proj_m = _in_proj(meta_chunk, norm_w, w_in_bf, tm=CHUNK, tn=1024)

    q, k, vt = _mla_prep(proj, qnw, kvnw, wq, wk, wvt, tabs64_x, tm=512, scale=scale)
    _, k_m, vt_m = _mla_prep(proj_m, qnw, kvnw, wq, wk, wvt, tabs64_m, tm=CHUNK, scale=scale)

    proj3 = proj.reshape(batch, seq, IN_PAD)
    y_mla = _attention(q.reshape(batch, seq, -1), k.reshape(batch, seq, -1), vt,
                       k_m, vt_m[0], proj3, tq=1024, tk=512)
    y_ret = _retention(proj3, proj_m, tabs128_x, tabs128_m, _decay_tables(),
                       ret_gn_w, ret_gn_b, group=8)

    out = _out_proj(y_mla.reshape(rows, -1), y_ret.reshape(rows, -1), proj, x2d,
                    w_branch_mla[0].astype(_BF), w_branch_ret[0].astype(_BF),
                    w_out[0].astype(_BF), final_norm_w.reshape(1, D_MODEL), tm=256)
    return out.reshape(batch, seq, D_MODEL)
```

```python
import functools

import numpy as np
import jax
import jax.numpy as jnp
from jax import lax
from jax.experimental import pallas as pl
from jax.experimental.pallas import tpu as pltpu

D_MODEL = 2048
N_META = 16
CHUNK = 128
META_PAD = CHUNK - N_META
HEADS = 8
NOPE = 128
ROPE = 64
HEAD_V = 128
Q_RANK = 512
KV_RANK = 256
QK_PAD = 256
ONES_ROWS = 16
RET_D = 128
ROPE_BASE = 10000.0
NORM_EPS = 1e-6
GN_EPS = 1e-5
NEG_INF = -1e30

CQ_OFF = 0
CKV_OFF = 512
KPE_OFF = 768
ZM_OFF = 1024
RQ_OFF = 2048
RK_OFF = 3072
RV_OFF = 4096
ZR_OFF = 5120
GATE_OFF = 6144
IN_PAD = 10240
KPE_END = KPE_OFF + ROPE

VMEM_LIMIT = 56 * 1024 * 1024

_BF = jnp.bfloat16
_F32 = jnp.float32


def _dot(a, b):
    return jnp.dot(a, b, preferred_element_type=_F32)


def _dot_nt(a, b):
    return lax.dot_general(a, b, (((1,), (1,)), ((), ())), preferred_element_type=_F32)


def _dot_tn(a, b):
    return lax.dot_general(a, b, (((0,), (0,)), ((), ())), preferred_element_type=_F32)


def _in_proj_kernel(x_ref, nw_ref, w_ref, o_ref, xn_ref, *, tn):
    j = pl.program_id(1)

    @pl.when(j == 0)
    def _():
        x = x_ref[...]
        ms = jnp.mean(x * x, axis=-1, keepdims=True)
        xn_ref[...] = (x * lax.rsqrt(ms + NORM_EPS) * nw_ref[...]).astype(_BF)

    acc = _dot(xn_ref[...], w_ref[...])
    col = j * tn
    is_gate = col >= GATE_OFF
    is_z = ((col >= ZM_OFF) & (col < RQ_OFF)) | ((col >= ZR_OFF) & (col < GATE_OFF))

    @pl.when(is_gate)
    def _():
        o_ref[...] = jax.nn.sigmoid(acc).astype(_BF)

    @pl.when(is_z)
    def _():
        o_ref[...] = (acc * jax.nn.sigmoid(acc)).astype(_BF)

    @pl.when(jnp.logical_not(is_gate | is_z))
    def _():
        o_ref[...] = acc.astype(_BF)


def _in_proj(x2d, norm_w, w_in_bf, *, tm, tn):
    rows = x2d.shape[0]
    assert all(off % tn == 0 for off in (ZM_OFF, RQ_OFF, ZR_OFF, GATE_OFF, IN_PAD))
    return pl.pallas_call(
        functools.partial(_in_proj_kernel, tn=tn),
        grid=(rows // tm, IN_PAD // tn),
        in_specs=[
            pl.BlockSpec((tm, D_MODEL), lambda i, j: (i, 0)),
            pl.BlockSpec((1, D_MODEL), lambda i, j: (0, 0)),
            pl.BlockSpec((D_MODEL, tn), lambda i, j: (0, j)),
        ],
        out_specs=pl.BlockSpec((tm, tn), lambda i, j: (i, j)),
        out_shape=jax.ShapeDtypeStruct((rows, IN_PAD), _BF),
        scratch_shapes=[pltpu.VMEM((tm, D_MODEL), _BF)],
        compiler_params=pltpu.CompilerParams(
            dimension_semantics=("parallel", "arbitrary"), vmem_limit_bytes=VMEM_LIMIT),
        name="in_proj",
    )(x2d, norm_w, w_in_bf)


def _rms(x, w):
    ms = jnp.mean(x * x, axis=-1, keepdims=True)
    return x * lax.rsqrt(ms + NORM_EPS) * w


def _rope64(t, c, s1, s2):
    return t * c + pltpu.roll(t, 96, 1) * s1 + pltpu.roll(t, 32, 1) * s2


def _mla_prep_kernel(cq_ref, ckv_ref, kpe_ref, qnw_ref, kvnw_ref, wq_ref, wk_ref, wvt_ref,
                     c_ref, s1_ref, s2_ref, q_ref, k_ref, vt_ref, *, scale):
    c, s1, s2 = c_ref[...], s1_ref[...], s2_ref[...]

    cqn = _rms(cq_ref[...].astype(_F32), qnw_ref[...]).astype(_BF)
    q = _dot(cqn, wq_ref[...])
    for h in range(HEADS):
        lo = h * QK_PAD
        q_ref[:, lo:lo + NOPE] = (q[:, lo:lo + NOPE] * scale).astype(_BF)
        q_ref[:, lo + NOPE:lo + QK_PAD] = (
            _rope64(q[:, lo + NOPE:lo + QK_PAD], c, s1, s2) * scale).astype(_BF)

    ckvn = _rms(ckv_ref[...].astype(_F32), kvnw_ref[...]).astype(_BF)
    kn = _dot(ckvn, wk_ref[...])
    kpe = _rope64(kpe_ref[...].astype(_F32), c, s1, s2).astype(_BF)
    for h in range(HEADS):
        lo = h * QK_PAD
        k_ref[:, lo:lo + NOPE] = kn[:, h * NOPE:(h + 1) * NOPE].astype(_BF)
        k_ref[:, lo + NOPE:lo + QK_PAD] = kpe
    vt_ref[0] = _dot_nt(wvt_ref[...], ckvn).astype(_BF)


def _mla_prep(proj, qnw, kvnw, wq, wk, wvt, tabs, *, tm, scale):
    rows = proj.shape[0]
    seq = tabs[0].shape[0]
    tab_blocks = seq // tm
    full = lambda shape: pl.BlockSpec(shape, lambda i: (0, 0))
    tab_spec = pl.BlockSpec((tm, 128), lambda i: (i % tab_blocks, 0))
    return pl.pallas_call(
        functools.partial(_mla_prep_kernel, scale=scale),
        grid=(rows // tm,),
        in_specs=[
            pl.BlockSpec((tm, Q_RANK), lambda i: (i, CQ_OFF // Q_RANK)),
            pl.BlockSpec((tm, KV_RANK), lambda i: (i, CKV_OFF // KV_RANK)),
            pl.BlockSpec((tm, 128), lambda i: (i, KPE_OFF // 128)),
            full((1, Q_RANK)), full((1, KV_RANK)),
            full((Q_RANK, HEADS * QK_PAD)), full((KV_RANK, HEADS * NOPE)),
            full((HEADS * HEAD_V, KV_RANK)),
            tab_spec, tab_spec, tab_spec,
        ],
        out_specs=[
            pl.BlockSpec((tm, HEADS * QK_PAD), lambda i: (i, 0)),
            pl.BlockSpec((tm, HEADS * QK_PAD), lambda i: (i, 0)),
            pl.BlockSpec((1, HEADS * HEAD_V, tm), lambda i: (i // tab_blocks, 0, i % tab_blocks)),
        ],
        out_shape=[
            jax.ShapeDtypeStruct((rows, HEADS * QK_PAD), _BF),
            jax.ShapeDtypeStruct((rows, HEADS * QK_PAD), _BF),
            jax.ShapeDtypeStruct((rows // seq, HEADS * HEAD_V, seq), _BF),
        ],
        compiler_params=pltpu.CompilerParams(
            dimension_semantics=("parallel",), vmem_limit_bytes=VMEM_LIMIT),
        name="mla_prep",
    )(proj, proj, proj, qnw, kvnw, wq, wk, wvt, *tabs)


def _attn_kernel(q_ref, k_ref, vt_ref, km_ref, vmt_ref, zs_ref, o_ref, m_ref, acc_ref,
                 sa_ref, sb_ref, ma_ref, mb_ref, *, seq, tq, tk):
    def ones_row(width):
        first = lax.broadcasted_iota(jnp.int32, (ONES_ROWS, width), 0) == 0
        return jnp.where(first, 1.0, 0.0).astype(_BF)

    def update(s, smax, vt, c0):
        m_old = m_ref[:, c0:]
        m_new = jnp.maximum(m_old, smax)
        alpha = jnp.exp2(m_old - m_new)
        p = jnp.exp2(s - m_new).astype(_BF)
        vt1 = jnp.concatenate([vt, ones_row(vt.shape[1])], axis=0)
        acc_ref[:, c0:] = alpha * acc_ref[:, c0:] + _dot(vt1, p)
        m_ref[:, c0:] = m_new

    meta_mask = lax.broadcasted_iota(jnp.int32, (CHUNK, tq), 0) >= META_PAD
    causal = (lax.broadcasted_iota(jnp.int32, (tk, tq), 0)
              <= lax.broadcasted_iota(jnp.int32, (tk, tq), 1))
    buf_a, buf_b = (sa_ref, ma_ref), (sb_ref, mb_ref)

    for qi in range(seq // tq):
        rows = slice(qi * tq, (qi + 1) * tq)
        nfull = qi * tq // tk
        nblk = nfull + tq // tk
        m_ref[...] = jnp.full((1, tq), NEG_INF, _F32)
        acc_ref[...] = jnp.zeros((HEAD_V + ONES_ROWS, tq), _F32)

        def diagonal(blk):
            return isinstance(blk, int) and blk >= nfull

        def col0(blk):
            return (blk - nfull) * tk if diagonal(blk) else 0

        def keys(blk):
            if isinstance(blk, int):
                return slice(blk * tk, (blk + 1) * tk)
            return pl.ds(pl.multiple_of(blk * tk, tk), tk)

        def produce(buf, blk):
            s_ref, smax_ref = buf
            c0 = col0(blk)
            s = _dot_nt(k_ref[0, keys(blk), :], q_ref[0, qi * tq + c0:(qi + 1) * tq, :])
            if diagonal(blk):
                s = jnp.where(causal[:, :tq - c0], s, NEG_INF)
            s_ref[:, c0:] = s
            smax_ref[:, c0:] = jnp.max(s, axis=0, keepdims=True)

        def consume(buf, blk):
            s_ref, smax_ref = buf
            c0 = col0(blk)
            update(s_ref[:, c0:], smax_ref[:, c0:], vt_ref[0, :, keys(blk)], c0)

        produce(buf_a, 0)
        s_meta = jnp.where(meta_mask, _dot_nt(km_ref[...], q_ref[0, rows, :]), NEG_INF)
        update(s_meta, jnp.max(s_meta, axis=0, keepdims=True), vmt_ref[...], 0)

        def pair(i, carry):
            produce(buf_b, 2 * i + 1)
            consume(buf_a, 2 * i)
            produce(buf_a, 2 * i + 2)
            consume(buf_b, 2 * i + 1)
            return carry

        trips = (nfull - 1) // 2 if nfull >= 3 else 0
        if trips:
            lax.fori_loop(0, trips, pair, 0)
        done = 2 * trips
        for blk in range(done, nblk - 1):
            produce(buf_b if blk % 2 == 0 else buf_a, blk + 1)
            consume(buf_a if blk % 2 == 0 else buf_b, blk)
        consume(buf_a if (nblk - 1) % 2 == 0 else buf_b, nblk - 1)

        y = acc_ref[:HEAD_V, :] * (1.0 / acc_ref[HEAD_V:HEAD_V + 1, :])
        o_ref[0, rows, :] = (y.T * zs_ref[0, rows, :].astype(_F32)).astype(_BF)


def _attention(q, k, vt, k_meta, vt_meta, proj3, *, tq, tk):
    batch, seq = q.shape[0], q.shape[1]
    return pl.pallas_call(
        functools.partial(_attn_kernel, seq=seq, tq=tq, tk=tk),
        grid=(batch, HEADS),
        in_specs=[
            pl.BlockSpec((1, seq, QK_PAD), lambda b, h: (b, 0, h)),
            pl.BlockSpec((1, seq, QK_PAD), lambda b, h: (b, 0, h)),
            pl.BlockSpec((1, HEAD_V, seq), lambda b, h: (b, h, 0)),
            pl.BlockSpec((CHUNK, QK_PAD), lambda b, h: (0, h)),
            pl.BlockSpec((HEAD_V, CHUNK), lambda b, h: (h, 0)),
            pl.BlockSpec((1, seq, HEAD_V), lambda b, h: (b, 0, ZM_OFF // HEAD_V + h)),
        ],
        out_specs=pl.BlockSpec((1, seq, HEAD_V), lambda b, h: (b, 0, h)),
        out_shape=jax.ShapeDtypeStruct((batch, seq, HEADS * HEAD_V), _BF),
        scratch_shapes=[pltpu.VMEM((1, tq), _F32),
                        pltpu.VMEM((HEAD_V + ONES_ROWS, tq), _F32),
                        pltpu.VMEM((tk, tq), _F32), pltpu.VMEM((tk, tq), _F32),
                        pltpu.VMEM((1, tq), _F32), pltpu.VMEM((1, tq), _F32)],
        compiler_params=pltpu.CompilerParams(
            dimension_semantics=("parallel", "parallel"), vmem_limit_bytes=VMEM_LIMIT),
        name="attention",
    )(q, k, vt, k_meta, vt_meta, proj3)


def _rope128(t, c, s):
    return t * c + pltpu.roll(t, 64, 1) * s


def _ret_kernel(rq_ref, rk_ref, rv_ref, zs_ref, rkm_ref, rvm_ref, c_ref, s_ref, cm_ref, sm_ref,
                dec_ref, zeta_ref, xi_ref, gch_ref, gw_ref, gb_ref, o_ref, *, seq, group):
    kscale = RET_D ** -0.5
    dec = dec_ref[0]
    zeta = zeta_ref[0]
    xi = xi_ref[0]
    gch = gch_ref[0]
    gw = gw_ref[...]
    gb = gb_ref[...]

    km = _rope128(rkm_ref[...].astype(_F32), cm_ref[...], sm_ref[...]) * kscale
    state0 = _dot_tn((km * zeta).astype(_BF), rvm_ref[...])

    rows = group * CHUNK
    bdims = ((0,), (0,))

    def body(gi, state):
        rs = pl.multiple_of(gi * rows, rows)
        c = c_ref[pl.ds(rs, rows), :]
        s = s_ref[pl.ds(rs, rows), :]
        q = _rope128(rq_ref[0, pl.ds(rs, rows), :].astype(_F32), c, s)
        k = _rope128(rk_ref[0, pl.ds(rs, rows), :].astype(_F32), c, s) * kscale
        q3 = q.reshape(group, CHUNK, RET_D)
        k3 = k.reshape(group, CHUNK, RET_D)
        v3 = rv_ref[0, pl.ds(rs, rows), :].reshape(group, CHUNK, RET_D)
        kz = jnp.swapaxes((k3 * zeta[None]).astype(_BF), 1, 2)
        kv = lax.dot_general(kz, v3, (((2,), (1,)), bdims), preferred_element_type=_F32)
        qb = q3.astype(_BF)
        sc = lax.dot_general(qb, k3.astype(_BF), (((2,), (2,)), bdims),
                             preferred_element_type=_F32) * dec[None]
        states = []
        for ci in range(group):
            states.append(state.astype(_BF))
            state = state * gch + kv[ci]
        o = (lax.dot_general(sc.astype(_BF), v3, (((2,), (1,)), bdims),
                             preferred_element_type=_F32)
             + lax.dot_general(qb, jnp.stack(states), (((2,), (1,)), bdims),
                               preferred_element_type=_F32) * xi[None])
        o = o.reshape(rows, RET_D)
        mu = jnp.mean(o, axis=-1, keepdims=True)
        d = o - mu
        var = jnp.mean(d * d, axis=-1, keepdims=True)
        y = d * lax.rsqrt(var + GN_EPS) * gw + gb
        o_ref[0, pl.ds(rs, rows), :] = (
            y * zs_ref[0, pl.ds(rs, rows), :].astype(_F32)).astype(_BF)
        return state

    lax.fori_loop(0, seq // rows, body, state0)


def _retention(proj3, proj_meta, tabs_x, tabs_m, consts, gn_w, gn_b, *, group):
    batch, seq = proj3.shape[0], proj3.shape[1]
    col = lambda off: (lambda b, h: (b, 0, off // RET_D + h))
    mcol = lambda off: (lambda b, h: (0, off // RET_D + h))
    head3 = pl.BlockSpec((1, CHUNK, RET_D), lambda b, h: (h, 0, 0))
    return pl.pallas_call(
        functools.partial(_ret_kernel, seq=seq, group=group),
        grid=(batch, HEADS),
        in_specs=[
            pl.BlockSpec((1, seq, RET_D), col(RQ_OFF)),
            pl.BlockSpec((1, seq, RET_D), col(RK_OFF)),
            pl.BlockSpec((1, seq, RET_D), col(RV_OFF)),
            pl.BlockSpec((1, seq, RET_D), col(ZR_OFF)),
            pl.BlockSpec((CHUNK, RET_D), mcol(RK_OFF)),
            pl.BlockSpec((CHUNK, RET_D), mcol(RV_OFF)),
            pl.BlockSpec((seq, RET_D), lambda b, h: (0, 0)),
            pl.BlockSpec((seq, RET_D), lambda b, h: (0, 0)),
            pl.BlockSpec((CHUNK, RET_D), lambda b, h: (0, 0)),
            pl.BlockSpec((CHUNK, RET_D), lambda b, h: (0, 0)),
            head3, head3, head3, head3,
            pl.BlockSpec((1, RET_D), lambda b, h: (0, h)),
            pl.BlockSpec((1, RET_D), lambda b, h: (0, h)),
        ],
        out_specs=pl.BlockSpec((1, seq, RET_D), lambda b, h: (b, 0, h)),
        out_shape=jax.ShapeDtypeStruct((batch, seq, HEADS * RET_D), _BF),
        compiler_params=pltpu.CompilerParams(
            dimension_semantics=("parallel", "parallel"), vmem_limit_bytes=VMEM_LIMIT),
        name="retention",
    )(proj3, proj3, proj3, proj3, proj_meta, proj_meta, *tabs_x, *tabs_m, *consts, gn_w, gn_b)


def _out_proj_kernel(ym_ref, yr_ref, g0_ref, g1_ref, x_ref, wbm_ref, wbr_ref, wo_ref, fw_ref, o_ref):
    m1 = _dot(ym_ref[...], wbm_ref[...])
    m2 = _dot(yr_ref[...], wbr_ref[...])
    merged = g0_ref[...].astype(_F32) * m1 + g1_ref[...].astype(_F32) * m2
    h = x_ref[...] + _dot(merged.astype(_BF), wo_ref[...])
    ms = jnp.mean(h * h, axis=-1, keepdims=True)
    o_ref[...] = h * lax.rsqrt(ms + NORM_EPS) * fw_ref[...]


def _out_proj(y_mla, y_ret, proj, x2d, wbm, wbr, wo, fw, *, tm):
    rows = x2d.shape[0]
    width = HEADS * HEAD_V
    const = lambda shape: pl.BlockSpec(shape, lambda i: (0, 0), pipeline_mode=pl.Buffered(1))
    return pl.pallas_call(
        _out_proj_kernel,
        grid=(rows // tm,),
        in_specs=[
            pl.BlockSpec((tm, width), lambda i: (i, 0)),
            pl.BlockSpec((tm, width), lambda i: (i, 0)),
            pl.BlockSpec((tm, D_MODEL), lambda i: (i, GATE_OFF // D_MODEL)),
            pl.BlockSpec((tm, D_MODEL), lambda i: (i, GATE_OFF // D_MODEL + 1)),
            pl.BlockSpec((tm, D_MODEL), lambda i: (i, 0)),
            const((width, D_MODEL)), const((width, D_MODEL)), const((D_MODEL, D_MODEL)),
            const((1, D_MODEL)),
        ],
        out_specs=pl.BlockSpec((tm, D_MODEL), lambda i: (i, 0)),
        out_shape=jax.ShapeDtypeStruct((rows, D_MODEL), _F32),
        compiler_params=pltpu.CompilerParams(
            dimension_semantics=("parallel",), vmem_limit_bytes=VMEM_LIMIT),
        name="out_proj",
    )(y_mla, y_ret, proj, proj, x2d, wbm, wbr, wo, fw)


def _rope_tables(seq):
    pos = jnp.arange(CHUNK + seq, dtype=_F32) - META_PAD
    z32 = jnp.zeros((pos.shape[0], 32), _F32)
    inv64 = ROPE_BASE ** (-jnp.arange(0, ROPE, 2, dtype=_F32) / ROPE)
    ang = pos[:, None] * inv64[None, :]
    cos, sin = jnp.cos(ang), jnp.sin(ang)
    c64 = jnp.concatenate([cos, cos, z32, z32], axis=1)
    s1 = jnp.concatenate([-sin, z32, z32, z32], axis=1)
    s2 = jnp.concatenate([z32, sin, z32, z32], axis=1)
    inv128 = ROPE_BASE ** (-jnp.arange(0, RET_D, 2, dtype=_F32) / RET_D)
    ang = pos[:, None] * inv128[None, :]
    cos, sin = jnp.cos(ang), jnp.sin(ang)
    c128 = jnp.concatenate([cos, cos], axis=1)
    s128 = jnp.concatenate([-sin, sin], axis=1)
    return c64, s1, s2, c128, s128


def _decay_tables():
    log_g = jnp.log1p(-(2.0 ** (-5.0 - jnp.arange(HEADS, dtype=_F32))))
    n = jnp.arange(CHUNK, dtype=_F32)
    diff = n[:, None] - n[None, :]
    dec = jnp.where(diff >= 0, jnp.exp(log_g[:, None, None] * jnp.maximum(diff, 0.0)), 0.0)
    ones = jnp.ones((1, 1, RET_D), _F32)
    zeta = jnp.exp(log_g[:, None] * (CHUNK - 1.0 - n))[:, :, None] * ones
    xi = jnp.exp(log_g[:, None] * (n + 1.0))[:, :, None] * ones
    gch = jnp.exp(log_g * CHUNK)[:, None, None] * jnp.ones((1, CHUNK, RET_D), _F32)
    return dec, zeta, xi, gch


def _pack_w_in_kernel(w_ref, o_ref):
    w = w_ref[...]
    o_ref[:, :KPE_END] = w[:, :KPE_END].astype(_BF)
    o_ref[:, KPE_END:ZM_OFF] = jnp.zeros((w.shape[0], ZM_OFF - KPE_END), _BF)
    o_ref[:, ZM_OFF:] = w[:, KPE_END:].astype(_BF)


def _pack_w_in(w_in, *, tr):
    width = w_in.shape[2]
    return pl.pallas_call(
        _pack_w_in_kernel,
        grid=(D_MODEL // tr,),
        in_specs=[pl.BlockSpec((None, tr, width), lambda i: (0, i, 0))],
        out_specs=pl.BlockSpec((tr, IN_PAD), lambda i: (i, 0)),
        out_shape=jax.ShapeDtypeStruct((D_MODEL, IN_PAD), _BF),
        compiler_params=pltpu.CompilerParams(
            dimension_semantics=("parallel",), vmem_limit_bytes=VMEM_LIMIT),
        name="pack_w_in",
    )(w_in)


def _pack_w_uq(w_uq):
    w = w_uq.reshape(Q_RANK, HEADS, NOPE + ROPE)
    w = jnp.pad(w, ((0, 0), (0, 0), (0, QK_PAD - NOPE - ROPE)))
    return w.reshape(Q_RANK, HEADS * QK_PAD).astype(_BF)


def _split_w_ukv(w_ukv):
    w = w_ukv.reshape(KV_RANK, HEADS, NOPE + HEAD_V)
    wk = w[:, :, :NOPE].reshape(KV_RANK, HEADS * NOPE)
    wvt = w[:, :, NOPE:].reshape(KV_RANK, HEADS * HEAD_V).T
    return wk.astype(_BF), wvt.astype(_BF)


def kernel(x, meta, norm_w, w_in, mla_q_norm_w, mla_w_uq, mla_kv_norm_w, mla_w_ukv, ret_gn_w,
           ret_gn_b, w_branch_mla, w_branch_ret, w_out, final_norm_w):
    batch, seq, _ = x.shape
    rows = batch * seq
    x2d = x.reshape(rows, D_MODEL)
    meta_chunk = jnp.pad(meta.astype(x.dtype), ((META_PAD, 0), (0, 0)))

    w_in_bf = _pack_w_in(w_in, tr=256)
    wq = _pack_w_uq(mla_w_uq[0])
    wk, wvt = _split_w_ukv(mla_w_ukv[0])
    qnw, kvnw = mla_q_norm_w, mla_kv_norm_w
    scale = float((NOPE + ROPE) ** -0.5 * np.log2(np.e))

    c64, s1, s2, c128, s128 = _rope_tables(seq)
    tabs64_x = (c64[CHUNK:], s1[CHUNK:], s2[CHUNK:])
    tabs64_m = (c64[:CHUNK], s1[:CHUNK], s2[:CHUNK])
    tabs128_x = (c128[CHUNK:], s128[CHUNK:])
    tabs128_m = (c128[:CHUNK], s128[:CHUNK])

    proj = _in_proj(x2d, norm_w, w_in_bf, tm=1024, tn=1024)
    proj_m = _in_proj(meta_chunk, norm_w, w_in_bf, tm=CHUNK, tn=1024)

    q, k, vt = _mla_prep(proj, qnw, kvnw, wq, wk, wvt, tabs64_x, tm=512, scale=scale)
    _, k_m, vt_m = _mla_prep(proj_m, qnw, kvnw, wq, wk, wvt, tabs64_m, tm=CHUNK, scale=scale)

    proj3 = proj.reshape(batch, seq, IN_PAD)
    y_mla = _attention(q.reshape(batch, seq, -1), k.reshape(batch, seq, -1), vt,
                       k_m, vt_m[0], proj3, tq=1024, tk=512)
    y_ret = _retention(proj3, proj_m, tabs128_x, tabs128_m, _decay_tables(),
                       ret_gn_w, ret_gn_b, group=8)

    out = _out_proj(y_mla.reshape(rows, -1), y_ret.reshape(rows, -1), proj, x2d,
                    w_branch_mla[0].astype(_BF), w_branch_ret[0].astype(_BF),
                    w_out[0].astype(_BF), final_norm_w.reshape(1, D_MODEL), tm=256)
    return out.reshape(batch, seq, D_MODEL)
```

```python
import functools

import numpy as np
import jax
import jax.numpy as jnp
from jax import lax
from jax.experimental import pallas as pl
from jax.experimental.pallas import tpu as pltpu

D_MODEL = 2048
N_META = 16
CHUNK = 128
META_PAD = CHUNK - N_META
HEADS = 8
NOPE = 128
ROPE = 64
HEAD_V = 128
Q_RANK = 512
KV_RANK = 256
QK_PAD = 256
ONES_ROWS = 16
RET_D = 128
ROPE_BASE = 10000.0
NORM_EPS = 1e-6
GN_EPS = 1e-5
NEG_INF = -1e30

CQ_OFF = 0
CKV_OFF = 512
KPE_OFF = 768
ZM_OFF = 1024
RQ_OFF = 2048
RK_OFF = 3072
RV_OFF = 4096
ZR_OFF = 5120
GATE_OFF = 6144
IN_PAD = 10240
KPE_END = KPE_OFF + ROPE

VMEM_LIMIT = 56 * 1024 * 1024

_BF = jnp.bfloat16
_F32 = jnp.float32


def _dot(a, b):
    return jnp.dot(a, b, preferred_element_type=_F32)


def _dot_nt(a, b):
    return lax.dot_general(a, b, (((1,), (1,)), ((), ())), preferred_element_type=_F32)


def _dot_tn(a, b):
    return lax.dot_general(a, b, (((0,), (0,)), ((), ())), preferred_element_type=_F32)


def _in_proj_kernel(x_ref, nw_ref, w_ref, o_ref, xn_ref, *, tn):
    j = pl.program_id(1)

    @pl.when(j == 0)
    def _():
        x = x_ref[...]
        ms = jnp.mean(x * x, axis=-1, keepdims=True)
        xn_ref[...] = (x * lax.rsqrt(ms + NORM_EPS) * nw_ref[...]).astype(_BF)

    acc = _dot_nt(xn_ref[...], w_ref[...])
    col = j * tn
    is_gate = col >= GATE_OFF
    is_z = ((col >= ZM_OFF) & (col < RQ_OFF)) | ((col >= ZR_OFF) & (col < GATE_OFF))

    @pl.when(is_gate)
    def _():
        o_ref[...] = jax.nn.sigmoid(acc).astype(_BF)

    @pl.when(is_z)
    def _():
        o_ref[...] = (acc * jax.nn.sigmoid(acc)).astype(_BF)

    @pl.when(jnp.logical_not(is_gate | is_z))
    def _():
        o_ref[...] = acc.astype(_BF)


def _in_proj(x2d, norm_w, w_in_bf, *, tm, tn):
    rows = x2d.shape[0]
    assert all(off % tn == 0 for off in (ZM_OFF, RQ_OFF, ZR_OFF, GATE_OFF, IN_PAD))
    return pl.pallas_call(
        functools.partial(_in_proj_kernel, tn=tn),
        grid=(rows // tm, IN_PAD // tn),
        in_specs=[
            pl.BlockSpec((tm, D_MODEL), lambda i, j: (i, 0)),
            pl.BlockSpec((1, D_MODEL), lambda i, j: (0, 0)),
            pl.BlockSpec((tn, D_MODEL), lambda i, j: (j, 0)),
        ],
        out_specs=pl.BlockSpec((tm, tn), lambda i, j: (i, j)),
        out_shape=jax.ShapeDtypeStruct((rows, IN_PAD), _BF),
        scratch_shapes=[pltpu.VMEM((tm, D_MODEL), _BF)],
        compiler_params=pltpu.CompilerParams(
            dimension_semantics=("parallel", "arbitrary"), vmem_limit_bytes=VMEM_LIMIT),
        name="in_proj",
    )(x2d, norm_w, w_in_bf)


def _rms(x, w):
    ms = jnp.mean(x * x, axis=-1, keepdims=True)
    return x * lax.rsqrt(ms + NORM_EPS) * w


def _rope64(t, c, s1, s2):
    return t * c + pltpu.roll(t, 96, 1) * s1 + pltpu.roll(t, 32, 1) * s2


def _mla_prep_kernel(cq_ref, ckv_ref, kpe_ref, qnw_ref, kvnw_ref, wq_ref, wk_ref, wvt_ref,
                     c_ref, s1_ref, s2_ref, q_ref, k_ref, vt_ref, *, scale):
    c, s1, s2 = c_ref[...], s1_ref[...], s2_ref[...]

    cqn = _rms(cq_ref[...].astype(_F32), qnw_ref[...]).astype(_BF)
    q = _dot(cqn, wq_ref[...])
    for h in range(HEADS):
        lo = h * QK_PAD
        q_ref[:, lo:lo + NOPE] = (q[:, lo:lo + NOPE] * scale).astype(_BF)
        q_ref[:, lo + NOPE:lo + QK_PAD] = (
            _rope64(q[:, lo + NOPE:lo + QK_PAD], c, s1, s2) * scale).astype(_BF)

    ckvn = _rms(ckv_ref[...].astype(_F32), kvnw_ref[...]).astype(_BF)
    kn = _dot(ckvn, wk_ref[...])
    kpe = _rope64(kpe_ref[...].astype(_F32), c, s1, s2).astype(_BF)
    for h in range(HEADS):
        lo = h * QK_PAD
        k_ref[:, lo:lo + NOPE] = kn[:, h * NOPE:(h + 1) * NOPE].astype(_BF)
        k_ref[:, lo + NOPE:lo + QK_PAD] = kpe
    vt_ref[0] = _dot_nt(wvt_ref[...], ckvn).astype(_BF)


def _mla_prep(proj, qnw, kvnw, wq, wk, wvt, tabs, *, tm, scale):
    rows = proj.shape[0]
    seq = tabs[0].shape[0]
    tab_blocks = seq // tm
    full = lambda shape: pl.BlockSpec(shape, lambda i: (0, 0))
    tab_spec = pl.BlockSpec((tm, 128), lambda i: (i % tab_blocks, 0))
    return pl.pallas_call(
        functools.partial(_mla_prep_kernel, scale=scale),
        grid=(rows // tm,),
        in_specs=[
            pl.BlockSpec((tm, Q_RANK), lambda i: (i, CQ_OFF // Q_RANK)),
            pl.BlockSpec((tm, KV_RANK), lambda i: (i, CKV_OFF // KV_RANK)),
            pl.BlockSpec((tm, 128), lambda i: (i, KPE_OFF // 128)),
            full((1, Q_RANK)), full((1, KV_RANK)),
            full((Q_RANK, HEADS * QK_PAD)), full((KV_RANK, HEADS * NOPE)),
            full((HEADS * HEAD_V, KV_RANK)),
            tab_spec, tab_spec, tab_spec,
        ],
        out_specs=[
            pl.BlockSpec((tm, HEADS * QK_PAD), lambda i: (i, 0)),
            pl.BlockSpec((tm, HEADS * QK_PAD), lambda i: (i, 0)),
            pl.BlockSpec((1, HEADS * HEAD_V, tm), lambda i: (i // tab_blocks, 0, i % tab_blocks)),
        ],
        out_shape=[
            jax.ShapeDtypeStruct((rows, HEADS * QK_PAD), _BF),
            jax.ShapeDtypeStruct((rows, HEADS * QK_PAD), _BF),
            jax.ShapeDtypeStruct((rows // seq, HEADS * HEAD_V, seq), _BF),
        ],
        compiler_params=pltpu.CompilerParams(
            dimension_semantics=("parallel",), vmem_limit_bytes=VMEM_LIMIT),
        name="mla_prep",
    )(proj, proj, proj, qnw, kvnw, wq, wk, wvt, *tabs)


def _attn_kernel(q_ref, k_ref, vt_ref, km_ref, vmt_ref, zs_ref, o_ref, m_ref, acc_ref,
                 sa_ref, sb_ref, ma_ref, mb_ref, *, seq, tq, tk):
    def ones_row(width):
        first = lax.broadcasted_iota(jnp.int32, (ONES_ROWS, width), 0) == 0
        return jnp.where(first, 1.0, 0.0).astype(_BF)

    def update(s, smax, vt, c0):
        m_old = m_ref[:, c0:]
        m_new = jnp.maximum(m_old, smax)
        alpha = jnp.exp2(m_old - m_new)
        p = jnp.exp2(s - m_new).astype(_BF)
        vt1 = jnp.concatenate([vt, ones_row(vt.shape[1])], axis=0)
        acc_ref[:, c0:] = alpha * acc_ref[:, c0:] + _dot(vt1, p)
        m_ref[:, c0:] = m_new

    meta_mask = lax.broadcasted_iota(jnp.int32, (CHUNK, tq), 0) >= META_PAD
    causal = (lax.broadcasted_iota(jnp.int32, (tk, tq), 0)
              <= lax.broadcasted_iota(jnp.int32, (tk, tq), 1))
    buf_a, buf_b = (sa_ref, ma_ref), (sb_ref, mb_ref)

    for qi in range(seq // tq):
        rows = slice(qi * tq, (qi + 1) * tq)
        nfull = qi * tq // tk
        nblk = nfull + tq // tk
        m_ref[...] = jnp.full((1, tq), NEG_INF, _F32)
        acc_ref[...] = jnp.zeros((HEAD_V + ONES_ROWS, tq), _F32)

        def diagonal(blk):
            return isinstance(blk, int) and blk >= nfull

        def col0(blk):
            return (blk - nfull) * tk if diagonal(blk) else 0

        def keys(blk):
            if isinstance(blk, int):
                return slice(blk * tk, (blk + 1) * tk)
            return pl.ds(pl.multiple_of(blk * tk, tk), tk)

        def produce(buf, blk):
            s_ref, smax_ref = buf
            c0 = col0(blk)
            s = _dot_nt(k_ref[0, keys(blk), :], q_ref[0, qi * tq + c0:(qi + 1) * tq, :])
            if diagonal(blk):
                s = jnp.where(causal[:, :tq - c0], s, NEG_INF)
            s_ref[:, c0:] = s
            smax_ref[:, c0:] = jnp.max(s, axis=0, keepdims=True)

        def consume(buf, blk):
            s_ref, smax_ref = buf
            c0 = col0(blk)
            update(s_ref[:, c0:], smax_ref[:, c0:], vt_ref[0, :, keys(blk)], c0)

        produce(buf_a, 0)
        s_meta = jnp.where(meta_mask, _dot_nt(km_ref[...], q_ref[0, rows, :]), NEG_INF)
        update(s_meta, jnp.max(s_meta, axis=0, keepdims=True), vmt_ref[...], 0)

        def pair(i, carry):
            produce(buf_b, 2 * i + 1)
            consume(buf_a, 2 * i)
            produce(buf_a, 2 * i + 2)
            consume(buf_b, 2 * i + 1)
            return carry

        trips = (nfull - 1) // 2 if nfull >= 3 else 0
        if trips:
            lax.fori_loop(0, trips, pair, 0)
        done = 2 * trips
        for blk in range(done, nblk - 1):
            produce(buf_b if blk % 2 == 0 else buf_a, blk + 1)
            consume(buf_a if blk % 2 == 0 else buf_b, blk)
        consume(buf_a if (nblk - 1) % 2 == 0 else buf_b, nblk - 1)

        y = acc_ref[:HEAD_V, :] * (1.0 / acc_ref[HEAD_V:HEAD_V + 1, :])
        o_ref[0, rows, :] = (y.T * zs_ref[0, rows, :].astype(_F32)).astype(_BF)


def _attention(q, k, vt, k_meta, vt_meta, proj3, *, tq, tk):
    batch, seq = q.shape[0], q.shape[1]
    return pl.pallas_call(
        functools.partial(_attn_kernel, seq=seq, tq=tq, tk=tk),
        grid=(batch, HEADS),
        in_specs=[
            pl.BlockSpec((1, seq, QK_PAD), lambda b, h: (b, 0, h)),
            pl.BlockSpec((1, seq, QK_PAD), lambda b, h: (b, 0, h)),
            pl.BlockSpec((1, HEAD_V, seq), lambda b, h: (b, h, 0)),
            pl.BlockSpec((CHUNK, QK_PAD), lambda b, h: (0, h)),
            pl.BlockSpec((HEAD_V, CHUNK), lambda b, h: (h, 0)),
            pl.BlockSpec((1, seq, HEAD_V), lambda b, h: (b, 0, ZM_OFF // HEAD_V + h)),
        ],
        out_specs=pl.BlockSpec((1, seq, HEAD_V), lambda b, h: (b, 0, h)),
        out_shape=jax.ShapeDtypeStruct((batch, seq, HEADS * HEAD_V), _BF),
        scratch_shapes=[pltpu.VMEM((1, tq), _F32),
                        pltpu.VMEM((HEAD_V + ONES_ROWS, tq), _F32),
                        pltpu.VMEM((tk, tq), _F32), pltpu.VMEM((tk, tq), _F32),
                        pltpu.VMEM((1, tq), _F32), pltpu.VMEM((1, tq), _F32)],
        compiler_params=pltpu.CompilerParams(
            dimension_semantics=("parallel", "parallel"), vmem_limit_bytes=VMEM_LIMIT),
        name="attention",
    )(q, k, vt, k_meta, vt_meta, proj3)


def _rope128(t, c, s):
    return t * c + pltpu.roll(t, 64, 1) * s


def _ret_kernel(rq_ref, rk_ref, rv_ref, zs_ref, rkm_ref, rvm_ref, c_ref, s_ref, cm_ref, sm_ref,
                dec_ref, zeta_ref, xi_ref, gch_ref, gw_ref, gb_ref, o_ref, *, seq, group):
    kscale = RET_D ** -0.5
    dec = dec_ref[0]
    zeta = zeta_ref[0]
    xi = xi_ref[0]
    gch = gch_ref[0]
    gw = gw_ref[...]
    gb = gb_ref[...]

    km = _rope128(rkm_ref[...].astype(_F32), cm_ref[...], sm_ref[...]) * kscale
    state0 = _dot_tn((km * zeta).astype(_BF), rvm_ref[...])

    rows = group * CHUNK
    bdims = ((0,), (0,))

    def body(gi, state):
        rs = pl.multiple_of(gi * rows, rows)
        c = c_ref[pl.ds(rs, rows), :]
        s = s_ref[pl.ds(rs, rows), :]
        q = _rope128(rq_ref[0, pl.ds(rs, rows), :].astype(_F32), c, s)
        k = _rope128(rk_ref[0, pl.ds(rs, rows), :].astype(_F32), c, s) * kscale
        q3 = q.reshape(group, CHUNK, RET_D)
        k3 = k.reshape(group, CHUNK, RET_D)
        v3 = rv_ref[0, pl.ds(rs, rows), :].reshape(group, CHUNK, RET_D)
        kz = jnp.swapaxes((k3 * zeta[None]).astype(_BF), 1, 2)
        kv = lax.dot_general(kz, v3, (((2,), (1,)), bdims), preferred_element_type=_F32)
        qb = q3.astype(_BF)
        sc = lax.dot_general(qb, k3.astype(_BF), (((2,), (2,)), bdims),
                             preferred_element_type=_F32) * dec[None]
        states = []
        for ci in range(group):
            states.append(state.astype(_BF))
            state = state * gch + kv[ci]
        o = (lax.dot_general(sc.astype(_BF), v3, (((2,), (1,)), bdims),
                             preferred_element_type=_F32)
             + lax.dot_general(qb, jnp.stack(states), (((2,), (1,)), bdims),
                               preferred_element_type=_F32) * xi[None])
        o = o.reshape(rows, RET_D)
        mu = jnp.mean(o, axis=-1, keepdims=True)
        d = o - mu
        var = jnp.mean(d * d, axis=-1, keepdims=True)
        y = d * lax.rsqrt(var + GN_EPS) * gw + gb
        o_ref[0, pl.ds(rs, rows), :] = (
            y * zs_ref[0, pl.ds(rs, rows), :].astype(_F32)).astype(_BF)
        return state

    lax.fori_loop(0, seq // rows, body, state0)


def _retention(proj3, proj_meta, tabs_x, tabs_m, consts, gn_w, gn_b, *, group):
    batch, seq = proj3.shape[0], proj3.shape[1]
    col = lambda off: (lambda b, h: (b, 0, off // RET_D + h))
    mcol = lambda off: (lambda b, h: (0, off // RET_D + h))
    head3 = pl.BlockSpec((1, CHUNK, RET_D), lambda b, h: (h, 0, 0))
    return pl.pallas_call(
        functools.partial(_ret_kernel, seq=seq, group=group),
        grid=(batch, HEADS),
        in_specs=[
            pl.BlockSpec((1, seq, RET_D), col(RQ_OFF)),
            pl.BlockSpec((1, seq, RET_D), col(RK_OFF)),
            pl.BlockSpec((1, seq, RET_D), col(RV_OFF)),
            pl.BlockSpec((1, seq, RET_D), col(ZR_OFF)),
            pl.BlockSpec((CHUNK, RET_D), mcol(RK_OFF)),
            pl.BlockSpec((CHUNK, RET_D), mcol(RV_OFF)),
            pl.BlockSpec((seq, RET_D), lambda b, h: (0, 0)),
            pl.BlockSpec((seq, RET_D), lambda b, h: (0, 0)),
            pl.BlockSpec((CHUNK, RET_D), lambda b, h: (0, 0)),
            pl.BlockSpec((CHUNK, RET_D), lambda b, h: (0, 0)),
            head3, head3, head3, head3,
            pl.BlockSpec((1, RET_D), lambda b, h: (0, h)),
            pl.BlockSpec((1, RET_D), lambda b, h: (0, h)),
        ],
        out_specs=pl.BlockSpec((1, seq, RET_D), lambda b, h: (b, 0, h)),
        out_shape=jax.ShapeDtypeStruct((batch, seq, HEADS * RET_D), _BF),
        compiler_params=pltpu.CompilerParams(
            dimension_semantics=("parallel", "parallel"), vmem_limit_bytes=VMEM_LIMIT),
        name="retention",
    )(proj3, proj3, proj3, proj3, proj_meta, proj_meta, *tabs_x, *tabs_m, *consts, gn_w, gn_b)


def _out_proj_kernel(ym_ref, yr_ref, g0_ref, g1_ref, x_ref, wbm_ref, wbr_ref, wo_ref, fw_ref, o_ref):
    m1 = _dot(ym_ref[...], wbm_ref[...])
    m2 = _dot(yr_ref[...], wbr_ref[...])
    merged = g0_ref[...].astype(_F32) * m1 + g1_ref[...].astype(_F32) * m2
    h = x_ref[...] + _dot(merged.astype(_BF), wo_ref[...])
    ms = jnp.mean(h * h, axis=-1, keepdims=True)
    o_ref[...] = h * lax.rsqrt(ms + NORM_EPS) * fw_ref[...]


def _out_proj(y_mla, y_ret, proj, x2d, wbm, wbr, wo, fw, *, tm):
    rows = x2d.shape[0]
    width = HEADS * HEAD_V
    const = lambda shape: pl.BlockSpec(shape, lambda i: (0, 0), pipeline_mode=pl.Buffered(1))
    return pl.pallas_call(
        _out_proj_kernel,
        grid=(rows // tm,),
        in_specs=[
            pl.BlockSpec((tm, width), lambda i: (i, 0)),
            pl.BlockSpec((tm, width), lambda i: (i, 0)),
            pl.BlockSpec((tm, D_MODEL), lambda i: (i, GATE_OFF // D_MODEL)),
            pl.BlockSpec((tm, D_MODEL), lambda i: (i, GATE_OFF // D_MODEL + 1)),
            pl.BlockSpec((tm, D_MODEL), lambda i: (i, 0)),
            const((width, D_MODEL)), const((width, D_MODEL)), const((D_MODEL, D_MODEL)),
            const((1, D_MODEL)),
        ],
        out_specs=pl.BlockSpec((tm, D_MODEL), lambda i: (i, 0)),
        out_shape=jax.ShapeDtypeStruct((rows, D_MODEL), _F32),
        compiler_params=pltpu.CompilerParams(
            dimension_semantics=("parallel",), vmem_limit_bytes=VMEM_LIMIT),
        name="out_proj",
    )(y_mla, y_ret, proj, proj, x2d, wbm, wbr, wo, fw)


def _rope_tables(seq):
    pos = jnp.arange(CHUNK + seq, dtype=_F32) - META_PAD
    z32 = jnp.zeros((pos.shape[0], 32), _F32)
    inv64 = ROPE_BASE ** (-jnp.arange(0, ROPE, 2, dtype=_F32) / ROPE)
    ang = pos[:, None] * inv64[None, :]
    cos, sin = jnp.cos(ang), jnp.sin(ang)
    c64 = jnp.concatenate([cos, cos, z32, z32], axis=1)
    s1 = jnp.concatenate([-sin, z32, z32, z32], axis=1)
    s2 = jnp.concatenate([z32, sin, z32, z32], axis=1)
    inv128 = ROPE_BASE ** (-jnp.arange(0, RET_D, 2, dtype=_F32) / RET_D)
    ang = pos[:, None] * inv128[None, :]
    cos, sin = jnp.cos(ang), jnp.sin(ang)
    c128 = jnp.concatenate([cos, cos], axis=1)
    s128 = jnp.concatenate([-sin, sin], axis=1)
    return c64, s1, s2, c128, s128


def _decay_tables():
    log_g = jnp.log1p(-(2.0 ** (-5.0 - jnp.arange(HEADS, dtype=_F32))))
    n = jnp.arange(CHUNK, dtype=_F32)
    diff = n[:, None] - n[None, :]
    dec = jnp.where(diff >= 0, jnp.exp(log_g[:, None, None] * jnp.maximum(diff, 0.0)), 0.0)
    ones = jnp.ones((1, 1, RET_D), _F32)
    zeta = jnp.exp(log_g[:, None] * (CHUNK - 1.0 - n))[:, :, None] * ones
    xi = jnp.exp(log_g[:, None] * (n + 1.0))[:, :, None] * ones
    gch = jnp.exp(log_g * CHUNK)[:, None, None] * jnp.ones((1, CHUNK, RET_D), _F32)
    return dec, zeta, xi, gch


def _pack_w_in_kernel(prev_ref, cur_ref, o_ref, *, tr):
    j = pl.program_id(0)
    pad = ZM_OFF - KPE_END

    @pl.when(j == 0)
    def _():
        o_ref[:KPE_END, :] = cur_ref[:KPE_END, :].astype(_BF)
        o_ref[KPE_END:, :] = jnp.zeros((tr - KPE_END, D_MODEL), _BF)

    @pl.when(j > 0)
    def _():
        o_ref[:pad, :] = prev_ref[tr - pad:, :].astype(_BF)
        o_ref[pad:, :] = cur_ref[:tr - pad, :].astype(_BF)


def _pack_w_in(w_in_t, *, tr):
    assert tr == ZM_OFF and IN_PAD % tr == 0
    steps = IN_PAD // tr
    width = w_in_t.shape[1]
    assert (steps - 1) * tr + tr - (ZM_OFF - KPE_END) == width
    return pl.pallas_call(
        functools.partial(_pack_w_in_kernel, tr=tr),
        grid=(steps,),
        in_specs=[
            pl.BlockSpec((None, tr, D_MODEL), lambda j: (0, jnp.maximum(j - 1, 0), 0)),
            pl.BlockSpec((None, tr, D_MODEL), lambda j: (0, j, 0)),
        ],
        out_specs=pl.BlockSpec((tr, D_MODEL), lambda j: (j, 0)),
        out_shape=jax.ShapeDtypeStruct((IN_PAD, D_MODEL), _BF),
        compiler_params=pltpu.CompilerParams(
            dimension_semantics=("parallel",), vmem_limit_bytes=VMEM_LIMIT),
        name="pack_w_in",
    )(w_in_t, w_in_t)


def _pack_w_uq(w_uq):
    w = w_uq.reshape(Q_RANK, HEADS, NOPE + ROPE)
    w = jnp.pad(w, ((0, 0), (0, 0), (0, QK_PAD - NOPE - ROPE)))
    return w.reshape(Q_RANK, HEADS * QK_PAD).astype(_BF)


def _split_w_ukv(w_ukv):
    w = w_ukv.reshape(KV_RANK, HEADS, NOPE + HEAD_V)
    wk = w[:, :, :NOPE].reshape(KV_RANK, HEADS * NOPE)
    wvt = w[:, :, NOPE:].reshape(KV_RANK, HEADS * HEAD_V).T
    return wk.astype(_BF), wvt.astype(_BF)


def kernel(x, meta, norm_w, w_in, mla_q_norm_w, mla_w_uq, mla_kv_norm_w, mla_w_ukv, ret_gn_w,
           ret_gn_b, w_branch_mla, w_branch_ret, w_out, final_norm_w):
    batch, seq, _ = x.shape
    rows = batch * seq
    x2d = x.reshape(rows, D_MODEL)
    meta_chunk = jnp.pad(meta.astype(x.dtype), ((META_PAD, 0), (0, 0)))

    w_in_bf = _pack_w_in(jnp.swapaxes(w_in, 1, 2), tr=ZM_OFF)
    wq = _pack_w_uq(mla_w_uq[0])
    wk, wvt = _split_w_ukv(mla_w_ukv[0])
    qnw, kvnw = mla_q_norm_w, mla_kv_norm_w
    scale = float((NOPE + ROPE) ** -0.5 * np.log2(np.e))

    c64, s1, s2, c128, s128 = _rope_tables(seq)
    tabs64_x = (c64[CHUNK:], s1[CHUNK:], s2[CHUNK:])
    tabs64_m = (c64[:CHUNK], s1[:CHUNK], s2[:CHUNK])
    tabs128_x = (c128[CHUNK:], s128[CHUNK:])
    tabs128_m = (c128[:CHUNK], s128[:CHUNK])

    proj = _in_proj(x2d, norm_w, w_in_bf, tm=1024, tn=1024)
    proj_m = _in_proj(meta_chunk, norm_w, w_in_bf, tm=CHUNK, tn=1024)

    q, k, vt = _mla_prep(proj, qnw, kvnw, wq, wk, wvt, tabs64_x, tm=512, scale=scale)
    _, k_m, vt_m = _mla_prep(proj_m, qnw, kvnw, wq, wk, wvt, tabs64_m, tm=CHUNK, scale=scale)

    proj3 = proj.reshape(batch, seq, IN_PAD)
    y_mla = _attention(q.reshape(batch, seq, -1), k.reshape(batch, seq, -1), vt,
                       k_m, vt_m[0], proj3, tq=1024, tk=512)
    y_ret = _retention(proj3, proj_m, tabs128_x, tabs128_m, _decay_tables(),
                       ret_gn_w, ret_gn_b, group=8)

    out = _out_proj(y_mla.reshape(rows, -1), y_ret.reshape(rows, -1), proj, x2d,
                    w_branch_mla[0].astype(_BF), w_branch_ret[0].astype(_BF),
                    w_out[0].astype(_BF), final_norm_w.reshape(1, D_MODEL), tm=256)
    return out.reshape(batch, seq, D_MODEL)
```

```python
import functools

import numpy as np
import jax
import jax.numpy as jnp
from jax import lax
from jax.experimental import pallas as pl
from jax.experimental.pallas import tpu as pltpu

D_MODEL = 2048
N_META = 16
CHUNK = 128
META_PAD = CHUNK - N_META
HEADS = 8
NOPE = 128
ROPE = 64
HEAD_V = 128
Q_RANK = 512
KV_RANK = 256
QK_PAD = 256
ONES_ROWS = 16
RET_D = 128
ROPE_BASE = 10000.0
NORM_EPS = 1e-6
GN_EPS = 1e-5
NEG_INF = -1e30

CQ_OFF = 0
CKV_OFF = 512
KPE_OFF = 768
ZM_OFF = 1024
RQ_OFF = 2048
RK_OFF = 3072
RV_OFF = 4096
ZR_OFF = 5120
GATE_OFF = 6144
IN_PAD = 10240
KPE_END = KPE_OFF + ROPE

VMEM_LIMIT = 56 * 1024 * 1024

_BF = jnp.bfloat16
_F32 = jnp.float32


def _dot(a, b):
    return jnp.dot(a, b, preferred_element_type=_F32)


def _dot_nt(a, b):
    return lax.dot_general(a, b, (((1,), (1,)), ((), ())), preferred_element_type=_F32)


def _dot_tn(a, b):
    return lax.dot_general(a, b, (((0,), (0,)), ((), ())), preferred_element_type=_F32)


def _sigmoid(x):
    return 0.5 * jnp.tanh(0.5 * x) + 0.5


def _in_proj_kernel(x_ref, nw_ref, w_ref, o_ref, xn_ref, *, tn):
    j = pl.program_id(1)

    @pl.when(j == 0)
    def _():
        x = x_ref[...]
        ms = jnp.mean(x * x, axis=-1, keepdims=True)
        xn_ref[...] = (x * lax.rsqrt(ms + NORM_EPS) * nw_ref[...]).astype(_BF)

    acc = _dot_nt(xn_ref[...], w_ref[...])
    col = j * tn
    is_gate = col >= GATE_OFF
    is_z = ((col >= ZM_OFF) & (col < RQ_OFF)) | ((col >= ZR_OFF) & (col < GATE_OFF))

    @pl.when(is_gate)
    def _():
        o_ref[...] = _sigmoid(acc).astype(_BF)

    @pl.when(is_z)
    def _():
        o_ref[...] = (acc * _sigmoid(acc)).astype(_BF)

    @pl.when(jnp.logical_not(is_gate | is_z))
    def _():
        o_ref[...] = acc.astype(_BF)


def _in_proj(x2d, norm_w, w_in_bf, *, tm, tn):
    rows = x2d.shape[0]
    assert all(off % tn == 0 for off in (ZM_OFF, RQ_OFF, ZR_OFF, GATE_OFF, IN_PAD))
    return pl.pallas_call(
        functools.partial(_in_proj_kernel, tn=tn),
        grid=(rows // tm, IN_PAD // tn),
        in_specs=[
            pl.BlockSpec((tm, D_MODEL), lambda i, j: (i, 0)),
            pl.BlockSpec((1, D_MODEL), lambda i, j: (0, 0)),
            pl.BlockSpec((tn, D_MODEL), lambda i, j: (j, 0)),
        ],
        out_specs=pl.BlockSpec((tm, tn), lambda i, j: (i, j)),
        out_shape=jax.ShapeDtypeStruct((rows, IN_PAD), _BF),
        scratch_shapes=[pltpu.VMEM((tm, D_MODEL), _BF)],
        compiler_params=pltpu.CompilerParams(
            dimension_semantics=("parallel", "arbitrary"), vmem_limit_bytes=VMEM_LIMIT),
        name="in_proj",
    )(x2d, norm_w, w_in_bf)


def _rms(x, w):
    ms = jnp.mean(x * x, axis=-1, keepdims=True)
    return x * lax.rsqrt(ms + NORM_EPS) * w


def _rope64(t, c, s1, s2):
    return t * c + pltpu.roll(t, 96, 1) * s1 + pltpu.roll(t, 32, 1) * s2


def _mla_prep_kernel(cq_ref, ckv_ref, kpe_ref, qnw_ref, kvnw_ref, wq_ref, wk_ref, wvt_ref,
                     c_ref, s1_ref, s2_ref, q_ref, k_ref, vt_ref, *, scale):
    c, s1, s2 = c_ref[...], s1_ref[...], s2_ref[...]

    cqn = _rms(cq_ref[...].astype(_F32), qnw_ref[...]).astype(_BF)
    q = _dot(cqn, wq_ref[...])
    for h in range(HEADS):
        lo = h * QK_PAD
        q_ref[:, lo:lo + NOPE] = (q[:, lo:lo + NOPE] * scale).astype(_BF)
        q_ref[:, lo + NOPE:lo + QK_PAD] = (
            _rope64(q[:, lo + NOPE:lo + QK_PAD], c, s1, s2) * scale).astype(_BF)

    ckvn = _rms(ckv_ref[...].astype(_F32), kvnw_ref[...]).astype(_BF)
    kn = _dot(ckvn, wk_ref[...])
    kpe = _rope64(kpe_ref[...].astype(_F32), c, s1, s2).astype(_BF)
    for h in range(HEADS):
        lo = h * QK_PAD
        k_ref[:, lo:lo + NOPE] = kn[:, h * NOPE:(h + 1) * NOPE].astype(_BF)
        k_ref[:, lo + NOPE:lo + QK_PAD] = kpe
    vt_ref[0] = _dot_nt(wvt_ref[...], ckvn).astype(_BF)


def _mla_prep(proj, qnw, kvnw, wq, wk, wvt, tabs, *, tm, scale):
    rows = proj.shape[0]
    seq = tabs[0].shape[0]
    tab_blocks = seq // tm
    full = lambda shape: pl.BlockSpec(shape, lambda i: (0, 0))
    tab_spec = pl.BlockSpec((tm, 128), lambda i: (i % tab_blocks, 0))
    return pl.pallas_call(
        functools.partial(_mla_prep_kernel, scale=scale),
        grid=(rows // tm,),
        in_specs=[
            pl.BlockSpec((tm, Q_RANK), lambda i: (i, CQ_OFF // Q_RANK)),
            pl.BlockSpec((tm, KV_RANK), lambda i: (i, CKV_OFF // KV_RANK)),
            pl.BlockSpec((tm, 128), lambda i: (i, KPE_OFF // 128)),
            full((1, Q_RANK)), full((1, KV_RANK)),
            full((Q_RANK, HEADS * QK_PAD)), full((KV_RANK, HEADS * NOPE)),
            full((HEADS * HEAD_V, KV_RANK)),
            tab_spec, tab_spec, tab_spec,
        ],
        out_specs=[
            pl.BlockSpec((tm, HEADS * QK_PAD), lambda i: (i, 0)),
            pl.BlockSpec((tm, HEADS * QK_PAD), lambda i: (i, 0)),
            pl.BlockSpec((1, HEADS * HEAD_V, tm), lambda i: (i // tab_blocks, 0, i % tab_blocks)),
        ],
        out_shape=[
            jax.ShapeDtypeStruct((rows, HEADS * QK_PAD), _BF),
            jax.ShapeDtypeStruct((rows, HEADS * QK_PAD), _BF),
            jax.ShapeDtypeStruct((rows // seq, HEADS * HEAD_V, seq), _BF),
        ],
        compiler_params=pltpu.CompilerParams(
            dimension_semantics=("parallel",), vmem_limit_bytes=VMEM_LIMIT),
        name="mla_prep",
    )(proj, proj, proj, qnw, kvnw, wq, wk, wvt, *tabs)


def _attn_kernel(q_ref, k_ref, vt_ref, km_ref, vmt_ref, zs_ref, o_ref, m_ref, acc_ref,
                 sa_ref, sb_ref, ma_ref, mb_ref, *, seq, tq, tk):
    def ones_row(width):
        first = lax.broadcasted_iota(jnp.int32, (ONES_ROWS, width), 0) == 0
        return jnp.where(first, 1.0, 0.0).astype(_BF)

    def update(s, smax, vt, c0):
        m_old = m_ref[:, c0:]
        m_new = jnp.maximum(m_old, smax)
        alpha = jnp.exp2(m_old - m_new)
        p = jnp.exp2(s - m_new).astype(_BF)
        if s.shape[0] < vt.shape[1]:
            p = jnp.concatenate(
                [p, jnp.zeros((vt.shape[1] - s.shape[0], s.shape[1]), _BF)], axis=0)
        vt1 = jnp.concatenate([vt, ones_row(vt.shape[1])], axis=0)
        acc_ref[:, c0:] = alpha * acc_ref[:, c0:] + _dot(vt1, p)
        m_ref[:, c0:] = m_new

    causal = (lax.broadcasted_iota(jnp.int32, (tk, tq), 0)
              <= lax.broadcasted_iota(jnp.int32, (tk, tq), 1))
    buf_a, buf_b = (sa_ref, ma_ref), (sb_ref, mb_ref)

    for qi in range(seq // tq):
        rows = slice(qi * tq, (qi + 1) * tq)
        nfull = qi * tq // tk
        nblk = nfull + tq // tk
        m_ref[...] = jnp.full((1, tq), NEG_INF, _F32)
        acc_ref[...] = jnp.zeros((HEAD_V + ONES_ROWS, tq), _F32)

        def diagonal(blk):
            return isinstance(blk, int) and blk >= nfull

        def col0(blk):
            return (blk - nfull) * tk if diagonal(blk) else 0

        def keys(blk):
            if isinstance(blk, int):
                return slice(blk * tk, (blk + 1) * tk)
            return pl.ds(pl.multiple_of(blk * tk, tk), tk)

        def produce(buf, blk):
            s_ref, smax_ref = buf
            c0 = col0(blk)
            s = _dot_nt(k_ref[0, keys(blk), :], q_ref[0, qi * tq + c0:(qi + 1) * tq, :])
            if diagonal(blk):
                s = jnp.where(causal[:, :tq - c0], s, NEG_INF)
            s_ref[:, c0:] = s
            smax_ref[:, c0:] = jnp.max(s, axis=0, keepdims=True)

        def consume(buf, blk):
            s_ref, smax_ref = buf
            c0 = col0(blk)
            update(s_ref[:, c0:], smax_ref[:, c0:], vt_ref[0, :, keys(blk)], c0)

        produce(buf_a, 0)
        s_meta = _dot_nt(km_ref[:N_META, :], q_ref[0, rows, :])
        update(s_meta, jnp.max(s_meta, axis=0, keepdims=True), vmt_ref[...], 0)

        def pair(i, carry):
            produce(buf_b, 2 * i + 1)
            consume(buf_a, 2 * i)
            produce(buf_a, 2 * i + 2)
            consume(buf_b, 2 * i + 1)
            return carry

        trips = (nfull - 1) // 2 if nfull >= 3 else 0
        if trips:
            lax.fori_loop(0, trips, pair, 0)
        done = 2 * trips
        for blk in range(done, nblk - 1):
            produce(buf_b if blk % 2 == 0 else buf_a, blk + 1)
            consume(buf_a if blk % 2 == 0 else buf_b, blk)
        consume(buf_a if (nblk - 1) % 2 == 0 else buf_b, nblk - 1)

        y = acc_ref[:HEAD_V, :] * (1.0 / acc_ref[HEAD_V:HEAD_V + 1, :])
        o_ref[0, rows, :] = (y.T * zs_ref[0, rows, :].astype(_F32)).astype(_BF)


def _attention(q, k, vt, k_meta, vt_meta, proj3, *, tq, tk):
    batch, seq = q.shape[0], q.shape[1]
    return pl.pallas_call(
        functools.partial(_attn_kernel, seq=seq, tq=tq, tk=tk),
        grid=(batch, HEADS),
        in_specs=[
            pl.BlockSpec((1, seq, QK_PAD), lambda b, h: (b, 0, h)),
            pl.BlockSpec((1, seq, QK_PAD), lambda b, h: (b, 0, h)),
            pl.BlockSpec((1, HEAD_V, seq), lambda b, h: (b, h, 0)),
            pl.BlockSpec((CHUNK, QK_PAD), lambda b, h: (0, h)),
            pl.BlockSpec((HEAD_V, CHUNK), lambda b, h: (h, 0)),
            pl.BlockSpec((1, seq, HEAD_V), lambda b, h: (b, 0, ZM_OFF // HEAD_V + h)),
        ],
        out_specs=pl.BlockSpec((1, seq, HEAD_V), lambda b, h: (b, 0, h)),
        out_shape=jax.ShapeDtypeStruct((batch, seq, HEADS * HEAD_V), _BF),
        scratch_shapes=[pltpu.VMEM((1, tq), _F32),
                        pltpu.VMEM((HEAD_V + ONES_ROWS, tq), _F32),
                        pltpu.VMEM((tk, tq), _F32), pltpu.VMEM((tk, tq), _F32),
                        pltpu.VMEM((1, tq), _F32), pltpu.VMEM((1, tq), _F32)],
        compiler_params=pltpu.CompilerParams(
            dimension_semantics=("parallel", "parallel"), vmem_limit_bytes=VMEM_LIMIT),
        name="attention",
    )(q, k, vt, k_meta, vt_meta, proj3)


def _rope128(t, c, s):
    return t * c + pltpu.roll(t, 64, 1) * s


def _ret_kernel(rq_ref, rk_ref, rv_ref, zs_ref, rkm_ref, rvm_ref, c_ref, s_ref, cm_ref, sm_ref,
                dec_ref, zeta_ref, xi_ref, gch_ref, gw_ref, gb_ref, o_ref, *, seq, group):
    kscale = RET_D ** -0.5
    dec = dec_ref[0]
    zeta = zeta_ref[0]
    xi = xi_ref[0]
    gch = gch_ref[0]
    gw = gw_ref[...]
    gb = gb_ref[...]

    km = _rope128(rkm_ref[...].astype(_F32), cm_ref[...], sm_ref[...]) * kscale
    state0 = _dot_tn((km * pltpu.roll(zeta, N_META, 0)).astype(_BF), rvm_ref[...])

    rows = group * CHUNK
    bdims = ((0,), (0,))

    def body(gi, state):
        rs = pl.multiple_of(gi * rows, rows)
        c = c_ref[pl.ds(rs, rows), :]
        s = s_ref[pl.ds(rs, rows), :]
        q = _rope128(rq_ref[0, pl.ds(rs, rows), :].astype(_F32), c, s)
        k = _rope128(rk_ref[0, pl.ds(rs, rows), :].astype(_F32), c, s) * kscale
        q3 = q.reshape(group, CHUNK, RET_D)
        k3 = k.reshape(group, CHUNK, RET_D)
        v3 = rv_ref[0, pl.ds(rs, rows), :].reshape(group, CHUNK, RET_D)
        kz = jnp.swapaxes((k3 * zeta[None]).astype(_BF), 1, 2)
        kv = lax.dot_general(kz, v3, (((2,), (1,)), bdims), preferred_element_type=_F32)
        qb = q3.astype(_BF)
        sc = lax.dot_general(qb, k3.astype(_BF), (((2,), (2,)), bdims),
                             preferred_element_type=_F32) * dec[None]
        states = []
        for ci in range(group):
            states.append(state.astype(_BF))
            state = state * gch + kv[ci]
        o = (lax.dot_general(sc.astype(_BF), v3, (((2,), (1,)), bdims),
                             preferred_element_type=_F32)
             + lax.dot_general(qb, jnp.stack(states), (((2,), (1,)), bdims),
                               preferred_element_type=_F32) * xi[None])
        o = o.reshape(rows, RET_D)
        mu = jnp.mean(o, axis=-1, keepdims=True)
        d = o - mu
        var = jnp.mean(d * d, axis=-1, keepdims=True)
        y = d * lax.rsqrt(var + GN_EPS) * gw + gb
        o_ref[0, pl.ds(rs, rows), :] = (
            y * zs_ref[0, pl.ds(rs, rows), :].astype(_F32)).astype(_BF)
        return state

    lax.fori_loop(0, seq // rows, body, state0)


def _retention(proj3, proj_meta, tabs_x, tabs_m, consts, gn_w, gn_b, *, group):
    batch, seq = proj3.shape[0], proj3.shape[1]
    col = lambda off: (lambda b, h: (b, 0, off // RET_D + h))
    mcol = lambda off: (lambda b, h: (0, off // RET_D + h))
    head3 = pl.BlockSpec((1, CHUNK, RET_D), lambda b, h: (h, 0, 0))
    return pl.pallas_call(
        functools.partial(_ret_kernel, seq=seq, group=group),
        grid=(batch, HEADS),
        in_specs=[
            pl.BlockSpec((1, seq, RET_D), col(RQ_OFF)),
            pl.BlockSpec((1, seq, RET_D), col(RK_OFF)),
            pl.BlockSpec((1, seq, RET_D), col(RV_OFF)),
            pl.BlockSpec((1, seq, RET_D), col(ZR_OFF)),
            pl.BlockSpec((CHUNK, RET_D), mcol(RK_OFF)),
            pl.BlockSpec((CHUNK, RET_D), mcol(RV_OFF)),
            pl.BlockSpec((seq, RET_D), lambda b, h: (0, 0)),
            pl.BlockSpec((seq, RET_D), lambda b, h: (0, 0)),
            pl.BlockSpec((CHUNK, RET_D), lambda b, h: (0, 0)),
            pl.BlockSpec((CHUNK, RET_D), lambda b, h: (0, 0)),
            head3, head3, head3, head3,
            pl.BlockSpec((1, RET_D), lambda b, h: (0, h)),
            pl.BlockSpec((1, RET_D), lambda b, h: (0, h)),
        ],
        out_specs=pl.BlockSpec((1, seq, RET_D), lambda b, h: (b, 0, h)),
        out_shape=jax.ShapeDtypeStruct((batch, seq, HEADS * RET_D), _BF),
        compiler_params=pltpu.CompilerParams(
            dimension_semantics=("parallel", "parallel"), vmem_limit_bytes=VMEM_LIMIT),
        name="retention",
    )(proj3, proj3, proj3, proj3, proj_meta, proj_meta, *tabs_x, *tabs_m, *consts, gn_w, gn_b)


def _out_proj_kernel(ym_ref, yr_ref, g0_ref, g1_ref, x_ref, wbm_ref, wbr_ref, wo_ref, fw_ref, o_ref):
    m1 = _dot(ym_ref[...], wbm_ref[...])
    m2 = _dot(yr_ref[...], wbr_ref[...])
    merged = g0_ref[...].astype(_F32) * m1 + g1_ref[...].astype(_F32) * m2
    h = x_ref[...] + _dot(merged.astype(_BF), wo_ref[...])
    ms = jnp.mean(h * h, axis=-1, keepdims=True)
    o_ref[...] = h * lax.rsqrt(ms + NORM_EPS) * fw_ref[...]


def _out_proj(y_mla, y_ret, proj, x2d, wbm, wbr, wo, fw, *, tm):
    rows = x2d.shape[0]
    width = HEADS * HEAD_V
    const = lambda shape: pl.BlockSpec(shape, lambda i: (0, 0), pipeline_mode=pl.Buffered(1))
    return pl.pallas_call(
        _out_proj_kernel,
        grid=(rows // tm,),
        in_specs=[
            pl.BlockSpec((tm, width), lambda i: (i, 0)),
            pl.BlockSpec((tm, width), lambda i: (i, 0)),
            pl.BlockSpec((tm, D_MODEL), lambda i: (i, GATE_OFF // D_MODEL)),
            pl.BlockSpec((tm, D_MODEL), lambda i: (i, GATE_OFF // D_MODEL + 1)),
            pl.BlockSpec((tm, D_MODEL), lambda i: (i, 0)),
            const((width, D_MODEL)), const((width, D_MODEL)), const((D_MODEL, D_MODEL)),
            const((1, D_MODEL)),
        ],
        out_specs=pl.BlockSpec((tm, D_MODEL), lambda i: (i, 0)),
        out_shape=jax.ShapeDtypeStruct((rows, D_MODEL), _F32),
        compiler_params=pltpu.CompilerParams(
            dimension_semantics=("parallel",), vmem_limit_bytes=VMEM_LIMIT),
        name="out_proj",
    )(y_mla, y_ret, proj, proj, x2d, wbm, wbr, wo, fw)


def _rope_tables(seq):
    pos = jnp.arange(CHUNK + seq, dtype=_F32) - META_PAD
    z32 = jnp.zeros((pos.shape[0], 32), _F32)
    inv64 = ROPE_BASE ** (-jnp.arange(0, ROPE, 2, dtype=_F32) / ROPE)
    ang = pos[:, None] * inv64[None, :]
    cos, sin = jnp.cos(ang), jnp.sin(ang)
    c64 = jnp.concatenate([cos, cos, z32, z32], axis=1)
    s1 = jnp.concatenate([-sin, z32, z32, z32], axis=1)
    s2 = jnp.concatenate([z32, sin, z32, z32], axis=1)
    inv128 = ROPE_BASE ** (-jnp.arange(0, RET_D, 2, dtype=_F32) / RET_D)
    ang = pos[:, None] * inv128[None, :]
    cos, sin = jnp.cos(ang), jnp.sin(ang)
    c128 = jnp.concatenate([cos, cos], axis=1)
    s128 = jnp.concatenate([-sin, sin], axis=1)
    return c64, s1, s2, c128, s128


def _decay_tables():
    log_g = jnp.log1p(-(2.0 ** (-5.0 - jnp.arange(HEADS, dtype=_F32))))
    n = jnp.arange(CHUNK, dtype=_F32)
    diff = n[:, None] - n[None, :]
    dec = jnp.where(diff >= 0, jnp.exp(log_g[:, None, None] * jnp.maximum(diff, 0.0)), 0.0)
    ones = jnp.ones((1, 1, RET_D), _F32)
    zeta = jnp.exp(log_g[:, None] * (CHUNK - 1.0 - n))[:, :, None] * ones
    xi = jnp.exp(log_g[:, None] * (n + 1.0))[:, :, None] * ones
    gch = jnp.exp(log_g * CHUNK)[:, None, None] * jnp.ones((1, CHUNK, RET_D), _F32)
    return dec, zeta, xi, gch


def _pack_w_in_kernel(prev_ref, cur_ref, o_ref, *, tr):
    j = pl.program_id(0)
    pad = ZM_OFF - KPE_END

    @pl.when(j == 0)
    def _():
        o_ref[:KPE_END, :] = cur_ref[:KPE_END, :].astype(_BF)
        o_ref[KPE_END:, :] = jnp.zeros((tr - KPE_END, D_MODEL), _BF)

    @pl.when(j > 0)
    def _():
        o_ref[:pad, :] = prev_ref[tr - pad:, :].astype(_BF)
        o_ref[pad:, :] = cur_ref[:tr - pad, :].astype(_BF)


def _pack_w_in(w_in_t, *, tr):
    assert tr == ZM_OFF and IN_PAD % tr == 0
    steps = IN_PAD // tr
    width = w_in_t.shape[1]
    assert (steps - 1) * tr + tr - (ZM_OFF - KPE_END) == width
    return pl.pallas_call(
        functools.partial(_pack_w_in_kernel, tr=tr),
        grid=(steps,),
        in_specs=[
            pl.BlockSpec((None, tr, D_MODEL), lambda j: (0, jnp.maximum(j - 1, 0), 0)),
            pl.BlockSpec((None, tr, D_MODEL), lambda j: (0, j, 0)),
        ],
        out_specs=pl.BlockSpec((tr, D_MODEL), lambda j: (j, 0)),
        out_shape=jax.ShapeDtypeStruct((IN_PAD, D_MODEL), _BF),
        compiler_params=pltpu.CompilerParams(
            dimension_semantics=("parallel",), vmem_limit_bytes=VMEM_LIMIT),
        name="pack_w_in",
    )(w_in_t, w_in_t)


def _pack_w_uq(w_uq):
    w = w_uq.reshape(Q_RANK, HEADS, NOPE + ROPE)
    w = jnp.pad(w, ((0, 0), (0, 0), (0, QK_PAD - NOPE - ROPE)))
    return w.reshape(Q_RANK, HEADS * QK_PAD).astype(_BF)


def _split_w_ukv(w_ukv):
    w = w_ukv.reshape(KV_RANK, HEADS, NOPE + HEAD_V)
    wk = w[:, :, :NOPE].reshape(KV_RANK, HEADS * NOPE)
    wvt = w[:, :, NOPE:].reshape(KV_RANK, HEADS * HEAD_V).T
    return wk.astype(_BF), wvt.astype(_BF)


def kernel(x, meta, norm_w, w_in, mla_q_norm_w, mla_w_uq, mla_kv_norm_w, mla_w_ukv, ret_gn_w,
           ret_gn_b, w_branch_mla, w_branch_ret, w_out, final_norm_w):
    batch, seq, _ = x.shape
    rows = batch * seq
    x2d = x.reshape(rows, D_MODEL)
    meta_chunk = jnp.pad(meta.astype(x.dtype), ((0, META_PAD), (0, 0)))

    w_in_bf = _pack_w_in(jnp.swapaxes(w_in, 1, 2), tr=ZM_OFF)
    wq = _pack_w_uq(mla_w_uq[0])
    wk, wvt = _split_w_ukv(mla_w_ukv[0])
    qnw, kvnw = mla_q_norm_w, mla_kv_norm_w
    scale = float((NOPE + ROPE) ** -0.5 * np.log2(np.e))

    c64, s1, s2, c128, s128 = _rope_tables(seq)
    meta_rows = slice(META_PAD, META_PAD + CHUNK)
    tabs64_x = (c64[CHUNK:], s1[CHUNK:], s2[CHUNK:])
    tabs64_m = (c64[meta_rows], s1[meta_rows], s2[meta_rows])
    tabs128_x = (c128[CHUNK:], s128[CHUNK:])
    tabs128_m = (c128[meta_rows], s128[meta_rows])

    proj = _in_proj(x2d, norm_w, w_in_bf, tm=1024, tn=1024)
    proj_m = _in_proj(meta_chunk, norm_w, w_in_bf, tm=CHUNK, tn=1024)

    q, k, vt = _mla_prep(proj, qnw, kvnw, wq, wk, wvt, tabs64_x, tm=512, scale=scale)
    _, k_m, vt_m = _mla_prep(proj_m, qnw, kvnw, wq, wk, wvt, tabs64_m, tm=CHUNK, scale=scale)

    proj3 = proj.reshape(batch, seq, IN_PAD)
    y_mla = _attention(q.reshape(batch, seq, -1), k.reshape(batch, seq, -1), vt,
                       k_m, vt_m[0], proj3, tq=1024, tk=512)
    y_ret = _retention(proj3, proj_m, tabs128_x, tabs128_m, _decay_tables(),
                       ret_gn_w, ret_gn_b, group=8)

    out = _out_proj(y_mla.reshape(rows, -1), y_ret.reshape(rows, -1), proj, x2d,
                    w_branch_mla[0].astype(_BF), w_branch_ret[0].astype(_BF),
                    w_out[0].astype(_BF), final_norm_w.reshape(1, D_MODEL), tm=256)
    return out.reshape(batch, seq, D_MODEL)
```

```python
import functools

import numpy as np
import jax
import jax.numpy as jnp
from jax import lax
from jax.experimental import pallas as pl
from jax.experimental.pallas import tpu as pltpu

D_MODEL = 2048
N_META = 16
CHUNK = 128
META_PAD = CHUNK - N_META
HEADS = 8
NOPE = 128
ROPE = 64
HEAD_V = 128
Q_RANK = 512
KV_RANK = 256
QK_PAD = 256
ONES_ROWS = 16
RET_D = 128
ROPE_BASE = 10000.0
NORM_EPS = 1e-6
GN_EPS = 1e-5
NEG_INF = -1e30

CQ_OFF = 0
CKV_OFF = 512
KPE_OFF = 768
ZM_OFF = 1024
RQ_OFF = 2048
RK_OFF = 3072
RV_OFF = 4096
ZR_OFF = 5120
GATE_OFF = 6144
IN_PAD = 10240
KPE_END = KPE_OFF + ROPE

VMEM_LIMIT = 56 * 1024 * 1024

_BF = jnp.bfloat16
_F32 = jnp.float32


def _dot(a, b):
    return jnp.dot(a, b, preferred_element_type=_F32)


def _dot_nt(a, b):
    return lax.dot_general(a, b, (((1,), (1,)), ((), ())), preferred_element_type=_F32)


def _dot_tn(a, b):
    return lax.dot_general(a, b, (((0,), (0,)), ((), ())), preferred_element_type=_F32)


def _sigmoid(x):
    return 0.5 * jnp.tanh(0.5 * x) + 0.5


def _in_proj_kernel(x_ref, nw_ref, w_ref, o_ref, xn_ref, *, tn):
    j = pl.program_id(1)

    @pl.when(j == 0)
    def _():
        x = x_ref[...]
        ms = jnp.mean(x * x, axis=-1, keepdims=True)
        xn_ref[...] = (x * lax.rsqrt(ms + NORM_EPS) * nw_ref[...]).astype(_BF)

    acc = _dot_nt(xn_ref[...], w_ref[...])
    col = j * tn
    is_gate = col >= GATE_OFF
    is_z = ((col >= ZM_OFF) & (col < RQ_OFF)) | ((col >= ZR_OFF) & (col < GATE_OFF))

    @pl.when(is_gate)
    def _():
        o_ref[...] = _sigmoid(acc).astype(_BF)

    @pl.when(is_z)
    def _():
        o_ref[...] = (acc * _sigmoid(acc)).astype(_BF)

    @pl.when(jnp.logical_not(is_gate | is_z))
    def _():
        o_ref[...] = acc.astype(_BF)


def _in_proj(x2d, norm_w, w_in_bf, *, tm, tn):
    rows = x2d.shape[0]
    assert all(off % tn == 0 for off in (ZM_OFF, RQ_OFF, ZR_OFF, GATE_OFF, IN_PAD))
    return pl.pallas_call(
        functools.partial(_in_proj_kernel, tn=tn),
        grid=(rows // tm, IN_PAD // tn),
        in_specs=[
            pl.BlockSpec((tm, D_MODEL), lambda i, j: (i, 0)),
            pl.BlockSpec((1, D_MODEL), lambda i, j: (0, 0)),
            pl.BlockSpec((tn, D_MODEL), lambda i, j: (j, 0)),
        ],
        out_specs=pl.BlockSpec((tm, tn), lambda i, j: (i, j)),
        out_shape=jax.ShapeDtypeStruct((rows, IN_PAD), _BF),
        scratch_shapes=[pltpu.VMEM((tm, D_MODEL), _BF)],
        compiler_params=pltpu.CompilerParams(
            dimension_semantics=("parallel", "arbitrary"), vmem_limit_bytes=VMEM_LIMIT),
        name="in_proj",
    )(x2d, norm_w, w_in_bf)


def _rms(x, w):
    ms = jnp.mean(x * x, axis=-1, keepdims=True)
    return x * lax.rsqrt(ms + NORM_EPS) * w


def _rope64(t, c, s1, s2):
    return t * c + pltpu.roll(t, 96, 1) * s1 + pltpu.roll(t, 32, 1) * s2


def _mla_prep_kernel(cq_ref, ckv_ref, kpe_ref, qnw_ref, kvnw_ref, wq_ref, wk_ref, wvt_ref,
                     c_ref, s1_ref, s2_ref, q_ref, k_ref, vt_ref, *, scale):
    c, s1, s2 = c_ref[...], s1_ref[...], s2_ref[...]

    cqn = _rms(cq_ref[...].astype(_F32), qnw_ref[...]).astype(_BF)
    q = _dot(cqn, wq_ref[...])
    for h in range(HEADS):
        lo = h * QK_PAD
        q_ref[:, lo:lo + NOPE] = (q[:, lo:lo + NOPE] * scale).astype(_BF)
        q_ref[:, lo + NOPE:lo + QK_PAD] = (
            _rope64(q[:, lo + NOPE:lo + QK_PAD], c, s1, s2) * scale).astype(_BF)

    ckvn = _rms(ckv_ref[...].astype(_F32), kvnw_ref[...]).astype(_BF)
    kn = _dot(ckvn, wk_ref[...])
    kpe = _rope64(kpe_ref[...].astype(_F32), c, s1, s2).astype(_BF)
    for h in range(HEADS):
        lo = h * QK_PAD
        k_ref[:, lo:lo + NOPE] = kn[:, h * NOPE:(h + 1) * NOPE].astype(_BF)
        k_ref[:, lo + NOPE:lo + QK_PAD] = kpe
    vt_ref[0] = _dot_nt(wvt_ref[...], ckvn).astype(_BF)


def _mla_prep(proj, qnw, kvnw, wq, wk, wvt, tabs, *, tm, scale):
    rows = proj.shape[0]
    seq = tabs[0].shape[0]
    tab_blocks = seq // tm
    full = lambda shape: pl.BlockSpec(shape, lambda i: (0, 0))
    tab_spec = pl.BlockSpec((tm, 128), lambda i: (i % tab_blocks, 0))
    return pl.pallas_call(
        functools.partial(_mla_prep_kernel, scale=scale),
        grid=(rows // tm,),
        in_specs=[
            pl.BlockSpec((tm, Q_RANK), lambda i: (i, CQ_OFF // Q_RANK)),
            pl.BlockSpec((tm, KV_RANK), lambda i: (i, CKV_OFF // KV_RANK)),
            pl.BlockSpec((tm, 128), lambda i: (i, KPE_OFF // 128)),
            full((1, Q_RANK)), full((1, KV_RANK)),
            full((Q_RANK, HEADS * QK_PAD)), full((KV_RANK, HEADS * NOPE)),
            full((HEADS * HEAD_V, KV_RANK)),
            tab_spec, tab_spec, tab_spec,
        ],
        out_specs=[
            pl.BlockSpec((tm, HEADS * QK_PAD), lambda i: (i, 0)),
            pl.BlockSpec((tm, HEADS * QK_PAD), lambda i: (i, 0)),
            pl.BlockSpec((1, HEADS * HEAD_V, tm), lambda i: (i // tab_blocks, 0, i % tab_blocks)),
        ],
        out_shape=[
            jax.ShapeDtypeStruct((rows, HEADS * QK_PAD), _BF),
            jax.ShapeDtypeStruct((rows, HEADS * QK_PAD), _BF),
            jax.ShapeDtypeStruct((rows // seq, HEADS * HEAD_V, seq), _BF),
        ],
        compiler_params=pltpu.CompilerParams(
            dimension_semantics=("parallel",), vmem_limit_bytes=VMEM_LIMIT),
        name="mla_prep",
    )(proj, proj, proj, qnw, kvnw, wq, wk, wvt, *tabs)


def _attn_kernel(q_ref, k_ref, vt_ref, km_ref, vmt_ref, zs_ref, o_ref, m_ref, acc_ref,
                 sa_ref, sb_ref, ma_ref, mb_ref, *, seq, tq, tk):
    def ones_row(width):
        first = lax.broadcasted_iota(jnp.int32, (ONES_ROWS, width), 0) == 0
        return jnp.where(first, 1.0, 0.0).astype(_BF)

    def update(s, smax, vt, c0):
        m_old = m_ref[:, c0:]
        m_new = jnp.maximum(m_old, smax)
        alpha = jnp.exp2(m_old - m_new)
        p = jnp.exp2(s - m_new).astype(_BF)
        if s.shape[0] < vt.shape[1]:
            p = jnp.concatenate(
                [p, jnp.zeros((vt.shape[1] - s.shape[0], s.shape[1]), _BF)], axis=0)
        vt1 = jnp.concatenate([vt, ones_row(vt.shape[1])], axis=0)
        acc_ref[:, c0:] = alpha * acc_ref[:, c0:] + _dot(vt1, p)
        m_ref[:, c0:] = m_new

    causal = (lax.broadcasted_iota(jnp.int32, (tk, tq), 0)
              <= lax.broadcasted_iota(jnp.int32, (tk, tq), 1))
    buf_a, buf_b = (sa_ref, ma_ref), (sb_ref, mb_ref)

    for qi in range(seq // tq):
        rows = slice(qi * tq, (qi + 1) * tq)
        nfull = qi * tq // tk
        nblk = nfull + tq // tk
        m_ref[...] = jnp.full((1, tq), NEG_INF, _F32)
        acc_ref[...] = jnp.zeros((HEAD_V + ONES_ROWS, tq), _F32)

        def diagonal(blk):
            return isinstance(blk, int) and blk >= nfull

        def col0(blk):
            return (blk - nfull) * tk if diagonal(blk) else 0

        def keys(blk):
            if isinstance(blk, int):
                return slice(blk * tk, (blk + 1) * tk)
            return pl.ds(pl.multiple_of(blk * tk, tk), tk)

        def produce(buf, blk):
            s_ref, smax_ref = buf
            c0 = col0(blk)
            s = _dot_nt(k_ref[0, keys(blk), :], q_ref[0, qi * tq + c0:(qi + 1) * tq, :])
            if diagonal(blk):
                s = jnp.where(causal[:, :tq - c0], s, NEG_INF)
            s_ref[:, c0:] = s
            smax_ref[:, c0:] = jnp.max(s, axis=0, keepdims=True)

        def consume(buf, blk):
            s_ref, smax_ref = buf
            c0 = col0(blk)
            update(s_ref[:, c0:], smax_ref[:, c0:], vt_ref[0, :, keys(blk)], c0)

        produce(buf_a, 0)
        s_meta = _dot_nt(km_ref[:N_META, :], q_ref[0, rows, :])
        update(s_meta, jnp.max(s_meta, axis=0, keepdims=True), vmt_ref[...], 0)

        def pair(i, carry):
            produce(buf_b, 2 * i + 1)
            consume(buf_a, 2 * i)
            produce(buf_a, 2 * i + 2)
            consume(buf_b, 2 * i + 1)
            return carry

        trips = (nfull - 1) // 2 if nfull >= 3 else 0
        if trips:
            lax.fori_loop(0, trips, pair, 0)
        done = 2 * trips
        for blk in range(done, nblk - 1):
            produce(buf_b if blk % 2 == 0 else buf_a, blk + 1)
            consume(buf_a if blk % 2 == 0 else buf_b, blk)
        consume(buf_a if (nblk - 1) % 2 == 0 else buf_b, nblk - 1)

        y = acc_ref[:HEAD_V, :] * (1.0 / acc_ref[HEAD_V:HEAD_V + 1, :])
        o_ref[0, rows, :] = (y.T * zs_ref[0, rows, :].astype(_F32)).astype(_BF)


def _attention(q, k, vt, k_meta, vt_meta, proj3, *, tq, tk):
    batch, seq = q.shape[0], q.shape[1]
    return pl.pallas_call(
        functools.partial(_attn_kernel, seq=seq, tq=tq, tk=tk),
        grid=(batch, HEADS),
        in_specs=[
            pl.BlockSpec((1, seq, QK_PAD), lambda b, h: (b, 0, h)),
            pl.BlockSpec((1, seq, QK_PAD), lambda b, h: (b, 0, h)),
            pl.BlockSpec((1, HEAD_V, seq), lambda b, h: (b, h, 0)),
            pl.BlockSpec((CHUNK, QK_PAD), lambda b, h: (0, h)),
            pl.BlockSpec((HEAD_V, CHUNK), lambda b, h: (h, 0)),
            pl.BlockSpec((1, seq, HEAD_V), lambda b, h: (b, 0, ZM_OFF // HEAD_V + h)),
        ],
        out_specs=pl.BlockSpec((1, seq, HEAD_V), lambda b, h: (b, 0, h)),
        out_shape=jax.ShapeDtypeStruct((batch, seq, HEADS * HEAD_V), _BF),
        scratch_shapes=[pltpu.VMEM((1, tq), _F32),
                        pltpu.VMEM((HEAD_V + ONES_ROWS, tq), _F32),
                        pltpu.VMEM((tk, tq), _F32), pltpu.VMEM((tk, tq), _F32),
                        pltpu.VMEM((1, tq), _F32), pltpu.VMEM((1, tq), _F32)],
        compiler_params=pltpu.CompilerParams(
            dimension_semantics=("parallel", "parallel"), vmem_limit_bytes=VMEM_LIMIT),
        name="attention",
    )(q, k, vt, k_meta, vt_meta, proj3)


def _rope128(t, c, s):
    return t * c + pltpu.roll(t, 64, 1) * s


def _ret_kernel(rq_ref, rk_ref, rv_ref, zs_ref, rkm_ref, rvm_ref, c_ref, s_ref, cm_ref, sm_ref,
                dec_ref, zeta_ref, xi_ref, gch_ref, gw_ref, gb_ref, o_ref, *, seq, group):
    kscale = RET_D ** -0.5
    dec = dec_ref[0]
    zeta = zeta_ref[0]
    xi = xi_ref[0]
    gch = gch_ref[0]
    gw = gw_ref[...]
    gb = gb_ref[...]

    km = _rope128(rkm_ref[...].astype(_F32), cm_ref[...], sm_ref[...]) * kscale
    state0 = _dot_tn((km * pltpu.roll(zeta, N_META, 0)).astype(_BF), rvm_ref[...])

    rows = group * CHUNK
    bdims = ((0,), (0,))

    def body(gi, state):
        rs = pl.multiple_of(gi * rows, rows)
        c = c_ref[pl.ds(rs, rows), :]
        s = s_ref[pl.ds(rs, rows), :]
        q = _rope128(rq_ref[0, pl.ds(rs, rows), :].astype(_F32), c, s)
        k = _rope128(rk_ref[0, pl.ds(rs, rows), :].astype(_F32), c, s) * kscale
        q3 = q.reshape(group, CHUNK, RET_D)
        k3 = k.reshape(group, CHUNK, RET_D)
        v3 = rv_ref[0, pl.ds(rs, rows), :].reshape(group, CHUNK, RET_D)
        kz = jnp.swapaxes((k3 * zeta[None]).astype(_BF), 1, 2)
        kv = lax.dot_general(kz, v3, (((2,), (1,)), bdims), preferred_element_type=_F32)
        qb = q3.astype(_BF)
        sc = lax.dot_general(qb, k3.astype(_BF), (((2,), (2,)), bdims),
                             preferred_element_type=_F32) * dec[None]
        states = []
        for ci in range(group):
            states.append(state.astype(_BF))
            state = state * gch + kv[ci]
        o = (lax.dot_general(sc.astype(_BF), v3, (((2,), (1,)), bdims),
                             preferred_element_type=_F32)
             + lax.dot_general(qb, jnp.stack(states), (((2,), (1,)), bdims),
                               preferred_element_type=_F32) * xi[None])
        o = o.reshape(rows, RET_D)
        mu = jnp.mean(o, axis=-1, keepdims=True)
        d = o - mu
        var = jnp.mean(d * d, axis=-1, keepdims=True)
        y = d * lax.rsqrt(var + GN_EPS) * gw + gb
        o_ref[0, pl.ds(rs, rows), :] = (
            y * zs_ref[0, pl.ds(rs, rows), :].astype(_F32)).astype(_BF)
        return state

    lax.fori_loop(0, seq // rows, body, state0)


def _retention(proj3, proj_meta, tabs_x, tabs_m, consts, gn_w, gn_b, *, group):
    batch, seq = proj3.shape[0], proj3.shape[1]
    col = lambda off: (lambda b, h: (b, 0, off // RET_D + h))
    mcol = lambda off: (lambda b, h: (0, off // RET_D + h))
    head3 = pl.BlockSpec((1, CHUNK, RET_D), lambda b, h: (h, 0, 0))
    return pl.pallas_call(
        functools.partial(_ret_kernel, seq=seq, group=group),
        grid=(batch, HEADS),
        in_specs=[
            pl.BlockSpec((1, seq, RET_D), col(RQ_OFF)),
            pl.BlockSpec((1, seq, RET_D), col(RK_OFF)),
            pl.BlockSpec((1, seq, RET_D), col(RV_OFF)),
            pl.BlockSpec((1, seq, RET_D), col(ZR_OFF)),
            pl.BlockSpec((CHUNK, RET_D), mcol(RK_OFF)),
            pl.BlockSpec((CHUNK, RET_D), mcol(RV_OFF)),
            pl.BlockSpec((seq, RET_D), lambda b, h: (0, 0)),
            pl.BlockSpec((seq, RET_D), lambda b, h: (0, 0)),
            pl.BlockSpec((CHUNK, RET_D), lambda b, h: (0, 0)),
            pl.BlockSpec((CHUNK, RET_D), lambda b, h: (0, 0)),
            head3, head3, head3, head3,
            pl.BlockSpec((1, RET_D), lambda b, h: (0, h)),
            pl.BlockSpec((1, RET_D), lambda b, h: (0, h)),
        ],
        out_specs=pl.BlockSpec((1, seq, RET_D), lambda b, h: (b, 0, h)),
        out_shape=jax.ShapeDtypeStruct((batch, seq, HEADS * RET_D), _BF),
        compiler_params=pltpu.CompilerParams(
            dimension_semantics=("parallel", "parallel"), vmem_limit_bytes=VMEM_LIMIT),
        name="retention",
    )(proj3, proj3, proj3, proj3, proj_meta, proj_meta, *tabs_x, *tabs_m, *consts, gn_w, gn_b)


def _out_proj_kernel(ym_ref, yr_ref, g0_ref, g1_ref, x_ref, wbm_ref, wbr_ref, wo_ref, fw_ref, o_ref):
    m1 = _dot(ym_ref[...], wbm_ref[...])
    m2 = _dot(yr_ref[...], wbr_ref[...])
    merged = g0_ref[...].astype(_F32) * m1 + g1_ref[...].astype(_F32) * m2
    h = x_ref[...] + _dot(merged.astype(_BF), wo_ref[...])
    ms = jnp.mean(h * h, axis=-1, keepdims=True)
    o_ref[...] = h * lax.rsqrt(ms + NORM_EPS) * fw_ref[...]


def _out_proj(y_mla, y_ret, proj, x2d, wbm, wbr, wo, fw, *, tm):
    rows = x2d.shape[0]
    width = HEADS * HEAD_V
    const = lambda shape: pl.BlockSpec(shape, lambda i: (0, 0), pipeline_mode=pl.Buffered(1))
    return pl.pallas_call(
        _out_proj_kernel,
        grid=(rows // tm,),
        in_specs=[
            pl.BlockSpec((tm, width), lambda i: (i, 0)),
            pl.BlockSpec((tm, width), lambda i: (i, 0)),
            pl.BlockSpec((tm, D_MODEL), lambda i: (i, GATE_OFF // D_MODEL)),
            pl.BlockSpec((tm, D_MODEL), lambda i: (i, GATE_OFF // D_MODEL + 1)),
            pl.BlockSpec((tm, D_MODEL), lambda i: (i, 0)),
            const((width, D_MODEL)), const((width, D_MODEL)), const((D_MODEL, D_MODEL)),
            const((1, D_MODEL)),
        ],
        out_specs=pl.BlockSpec((tm, D_MODEL), lambda i: (i, 0)),
        out_shape=jax.ShapeDtypeStruct((rows, D_MODEL), _F32),
        compiler_params=pltpu.CompilerParams(
            dimension_semantics=("parallel",), vmem_limit_bytes=VMEM_LIMIT),
        name="out_proj",
    )(y_mla, y_ret, proj, proj, x2d, wbm, wbr, wo, fw)


def _rope_tables(seq):
    pos = np.arange(CHUNK + seq, dtype=np.float64) - META_PAD
    z32 = np.zeros((pos.shape[0], 32))
    inv64 = ROPE_BASE ** (-np.arange(0, ROPE, 2, dtype=np.float64) / ROPE)
    ang = pos[:, None] * inv64[None, :]
    cos, sin = np.cos(ang), np.sin(ang)
    c64 = np.concatenate([cos, cos, z32, z32], axis=1)
    s1 = np.concatenate([-sin, z32, z32, z32], axis=1)
    s2 = np.concatenate([z32, sin, z32, z32], axis=1)
    inv128 = ROPE_BASE ** (-np.arange(0, RET_D, 2, dtype=np.float64) / RET_D)
    ang = pos[:, None] * inv128[None, :]
    cos, sin = np.cos(ang), np.sin(ang)
    c128 = np.concatenate([cos, cos], axis=1)
    s128 = np.concatenate([-sin, sin], axis=1)
    return tuple(t.astype(np.float32) for t in (c64, s1, s2, c128, s128))


def _decay_tables():
    log_g = jnp.log1p(-(2.0 ** (-5.0 - jnp.arange(HEADS, dtype=_F32))))
    n = jnp.arange(CHUNK, dtype=_F32)
    diff = n[:, None] - n[None, :]
    dec = jnp.where(diff >= 0, jnp.exp(log_g[:, None, None] * jnp.maximum(diff, 0.0)), 0.0)
    ones = jnp.ones((1, 1, RET_D), _F32)
    zeta = jnp.exp(log_g[:, None] * (CHUNK - 1.0 - n))[:, :, None] * ones
    xi = jnp.exp(log_g[:, None] * (n + 1.0))[:, :, None] * ones
    gch = jnp.exp(log_g * CHUNK)[:, None, None] * jnp.ones((1, CHUNK, RET_D), _F32)
    return dec, zeta, xi, gch


def _pack_w_in_kernel(w_ref, o_ref, tail_ref, *, tr):
    j = pl.program_id(0)
    pad = ZM_OFF - KPE_END

    @pl.when(j == 0)
    def _():
        o_ref[:KPE_END, :] = w_ref[:KPE_END, :].astype(_BF)
        o_ref[KPE_END:, :] = jnp.zeros((tr - KPE_END, D_MODEL), _BF)

    @pl.when(j > 0)
    def _():
        o_ref[:pad, :] = tail_ref[...]
        o_ref[pad:, :] = w_ref[:tr - pad, :].astype(_BF)

    @pl.when(j + 1 < pl.num_programs(0))
    def _():
        tail_ref[...] = w_ref[tr - pad:, :].astype(_BF)


def _pack_w_in(w_in_t, *, tr):
    assert tr == ZM_OFF and IN_PAD % tr == 0
    steps = IN_PAD // tr
    width = w_in_t.shape[1]
    assert (steps - 1) * tr + tr - (ZM_OFF - KPE_END) == width
    return pl.pallas_call(
        functools.partial(_pack_w_in_kernel, tr=tr),
        grid=(steps,),
        in_specs=[pl.BlockSpec((None, tr, D_MODEL), lambda j: (0, j, 0))],
        out_specs=pl.BlockSpec((tr, D_MODEL), lambda j: (j, 0)),
        out_shape=jax.ShapeDtypeStruct((IN_PAD, D_MODEL), _BF),
        scratch_shapes=[pltpu.VMEM((ZM_OFF - KPE_END, D_MODEL), _BF)],
        compiler_params=pltpu.CompilerParams(
            dimension_semantics=("arbitrary",), vmem_limit_bytes=VMEM_LIMIT),
        name="pack_w_in",
    )(w_in_t)


def _pack_w_uq(w_uq):
    w = w_uq.reshape(Q_RANK, HEADS, NOPE + ROPE)
    w = jnp.pad(w, ((0, 0), (0, 0), (0, QK_PAD - NOPE - ROPE)))
    return w.reshape(Q_RANK, HEADS * QK_PAD).astype(_BF)


def _split_w_ukv(w_ukv):
    w = w_ukv.reshape(KV_RANK, HEADS, NOPE + HEAD_V)
    wk = w[:, :, :NOPE].reshape(KV_RANK, HEADS * NOPE)
    wvt = w[:, :, NOPE:].reshape(KV_RANK, HEADS * HEAD_V).T
    return wk.astype(_BF), wvt.astype(_BF)


def kernel(x, meta, norm_w, w_in, mla_q_norm_w, mla_w_uq, mla_kv_norm_w, mla_w_ukv, ret_gn_w,
           ret_gn_b, w_branch_mla, w_branch_ret, w_out, final_norm_w):
    batch, seq, _ = x.shape
    rows = batch * seq
    x2d = x.reshape(rows, D_MODEL)
    meta_chunk = jnp.pad(meta.astype(x.dtype), ((0, META_PAD), (0, 0)))

    w_in_bf = _pack_w_in(jnp.swapaxes(w_in, 1, 2), tr=ZM_OFF)
    wq = _pack_w_uq(mla_w_uq[0])
    wk, wvt = _split_w_ukv(mla_w_ukv[0])
    qnw, kvnw = mla_q_norm_w, mla_kv_norm_w
    scale = float((NOPE + ROPE) ** -0.5 * np.log2(np.e))

    c64, s1, s2, c128, s128 = _rope_tables(seq)
    meta_rows = slice(META_PAD, META_PAD + CHUNK)
    tabs64_x = (c64[CHUNK:], s1[CHUNK:], s2[CHUNK:])
    tabs64_m = (c64[meta_rows], s1[meta_rows], s2[meta_rows])
    tabs128_x = (c128[CHUNK:], s128[CHUNK:])
    tabs128_m = (c128[meta_rows], s128[meta_rows])

    proj = _in_proj(x2d, norm_w, w_in_bf, tm=1024, tn=1024)
    proj_m = _in_proj(meta_chunk, norm_w, w_in_bf, tm=CHUNK, tn=1024)

    q, k, vt = _mla_prep(proj, qnw, kvnw, wq, wk, wvt, tabs64_x, tm=512, scale=scale)
    _, k_m, vt_m = _mla_prep(proj_m, qnw, kvnw, wq, wk, wvt, tabs64_m, tm=CHUNK, scale=scale)

    proj3 = proj.reshape(batch, seq, IN_PAD)
    y_mla = _attention(q.reshape(batch, seq, -1), k.reshape(batch, seq, -1), vt,
                       k_m, vt_m[0], proj3, tq=1024, tk=512)
    y_ret = _retention(proj3, proj_m, tabs128_x, tabs128_m, _decay_tables(),
                       ret_gn_w, ret_gn_b, group=16)

    out = _out_proj(y_mla.reshape(rows, -1), y_ret.reshape(rows, -1), proj, x2d,
                    w_branch_mla[0].astype(_BF), w_branch_ret[0].astype(_BF),
                    w_out[0].astype(_BF), final_norm_w.reshape(1, D_MODEL), tm=256)
    return out.reshape(batch, seq, D_MODEL)
```

```python
import functools

import numpy as np
import jax
import jax.numpy as jnp
from jax import lax
from jax.experimental import pallas as pl
from jax.experimental.pallas import tpu as pltpu

D_MODEL = 2048
N_META = 16
CHUNK = 128
META_PAD = CHUNK - N_META
HEADS = 8
NOPE = 128
ROPE = 64
HEAD_V = 128
Q_RANK = 512
KV_RANK = 256
QK_PAD = 256
ONES_ROWS = 16
RET_D = 128
ROPE_BASE = 10000.0
NORM_EPS = 1e-6
GN_EPS = 1e-5
NEG_INF = -1e30

CQ_OFF = 0
CKV_OFF = 512
KPE_OFF = 768
ZM_OFF = 1024
RQ_OFF = 2048
RK_OFF = 3072
RV_OFF = 4096
ZR_OFF = 5120
GATE_OFF = 6144
IN_PAD = 10240
KPE_END = KPE_OFF + ROPE

VMEM_LIMIT = 56 * 1024 * 1024

_BF = jnp.bfloat16
_F32 = jnp.float32


def _dot(a, b):
    return jnp.dot(a, b, preferred_element_type=_F32)


def _dot_nt(a, b):
    return lax.dot_general(a, b, (((1,), (1,)), ((), ())), preferred_element_type=_F32)


def _dot_tn(a, b):
    return lax.dot_general(a, b, (((0,), (0,)), ((), ())), preferred_element_type=_F32)


def _sigmoid(x):
    return 0.5 * jnp.tanh(0.5 * x) + 0.5


def _in_proj_kernel(x_ref, nw_ref, w_ref, o_ref, xn_ref, *, tn):
    j = pl.program_id(1)

    @pl.when(j == 0)
    def _():
        x = x_ref[...]
        ms = jnp.mean(x * x, axis=-1, keepdims=True)
        xn_ref[...] = (x * lax.rsqrt(ms + NORM_EPS) * nw_ref[...]).astype(_BF)

    acc = _dot_nt(xn_ref[...], w_ref[...])
    col = j * tn
    is_gate = col >= GATE_OFF
    is_z = ((col >= ZM_OFF) & (col < RQ_OFF)) | ((col >= ZR_OFF) & (col < GATE_OFF))

    @pl.when(is_gate)
    def _():
        o_ref[...] = _sigmoid(acc).astype(_BF)

    @pl.when(is_z)
    def _():
        o_ref[...] = (acc * _sigmoid(acc)).astype(_BF)

    @pl.when(jnp.logical_not(is_gate | is_z))
    def _():
        o_ref[...] = acc.astype(_BF)


def _in_proj(x2d, norm_w, w_in_bf, *, tm, tn):
    rows = x2d.shape[0]
    assert all(off % tn == 0 for off in (ZM_OFF, RQ_OFF, ZR_OFF, GATE_OFF, IN_PAD))
    return pl.pallas_call(
        functools.partial(_in_proj_kernel, tn=tn),
        grid=(rows // tm, IN_PAD // tn),
        in_specs=[
            pl.BlockSpec((tm, D_MODEL), lambda i, j: (i, 0)),
            pl.BlockSpec((1, D_MODEL), lambda i, j: (0, 0)),
            pl.BlockSpec((tn, D_MODEL), lambda i, j: (j, 0)),
        ],
        out_specs=pl.BlockSpec((tm, tn), lambda i, j: (i, j)),
        out_shape=jax.ShapeDtypeStruct((rows, IN_PAD), _BF),
        scratch_shapes=[pltpu.VMEM((tm, D_MODEL), _BF)],
        compiler_params=pltpu.CompilerParams(
            dimension_semantics=("parallel", "arbitrary"), vmem_limit_bytes=VMEM_LIMIT),
        name="in_proj",
    )(x2d, norm_w, w_in_bf)


def _rms(x, w):
    ms = jnp.mean(x * x, axis=-1, keepdims=True)
    return x * lax.rsqrt(ms + NORM_EPS) * w


def _rope64(t, c, s1, s2):
    return t * c + pltpu.roll(t, 96, 1) * s1 + pltpu.roll(t, 32, 1) * s2


def _mla_prep_kernel(cq_ref, ckv_ref, kpe_ref, qnw_ref, kvnw_ref, wq_ref, wk_ref, wvt_ref,
                     c_ref, s1_ref, s2_ref, q_ref, k_ref, vt_ref, *, scale):
    c, s1, s2 = c_ref[...], s1_ref[...], s2_ref[...]

    cqn = _rms(cq_ref[...].astype(_F32), qnw_ref[...]).astype(_BF)
    q = _dot(cqn, wq_ref[...])
    for h in range(HEADS):
        lo = h * QK_PAD
        q_ref[0, h, :, :NOPE] = (q[:, lo:lo + NOPE] * scale).astype(_BF)
        q_ref[0, h, :, NOPE:] = (
            _rope64(q[:, lo + NOPE:lo + QK_PAD], c, s1, s2) * scale).astype(_BF)

    ckvn = _rms(ckv_ref[...].astype(_F32), kvnw_ref[...]).astype(_BF)
    kn = _dot(ckvn, wk_ref[...])
    kpe = _rope64(kpe_ref[...].astype(_F32), c, s1, s2).astype(_BF)
    for h in range(HEADS):
        k_ref[0, h, :, :NOPE] = kn[:, h * NOPE:(h + 1) * NOPE].astype(_BF)
        k_ref[0, h, :, NOPE:] = kpe
    vt_ref[0] = _dot_nt(wvt_ref[...], ckvn).astype(_BF)


def _mla_prep(proj, qnw, kvnw, wq, wk, wvt, tabs, *, tm, scale):
    rows = proj.shape[0]
    seq = tabs[0].shape[0]
    tab_blocks = seq // tm
    full = lambda shape: pl.BlockSpec(shape, lambda i: (0, 0))
    tab_spec = pl.BlockSpec((tm, 128), lambda i: (i % tab_blocks, 0))
    return pl.pallas_call(
        functools.partial(_mla_prep_kernel, scale=scale),
        grid=(rows // tm,),
        in_specs=[
            pl.BlockSpec((tm, Q_RANK), lambda i: (i, CQ_OFF // Q_RANK)),
            pl.BlockSpec((tm, KV_RANK), lambda i: (i, CKV_OFF // KV_RANK)),
            pl.BlockSpec((tm, 128), lambda i: (i, KPE_OFF // 128)),
            full((1, Q_RANK)), full((1, KV_RANK)),
            full((Q_RANK, HEADS * QK_PAD)), full((KV_RANK, HEADS * NOPE)),
            full((HEADS * HEAD_V, KV_RANK)),
            tab_spec, tab_spec, tab_spec,
        ],
        out_specs=[
            pl.BlockSpec((1, HEADS, tm, QK_PAD), lambda i: (i // tab_blocks, 0, i % tab_blocks, 0)),
            pl.BlockSpec((1, HEADS, tm, QK_PAD), lambda i: (i // tab_blocks, 0, i % tab_blocks, 0)),
            pl.BlockSpec((1, HEADS * HEAD_V, tm), lambda i: (i // tab_blocks, 0, i % tab_blocks)),
        ],
        out_shape=[
            jax.ShapeDtypeStruct((rows // seq, HEADS, seq, QK_PAD), _BF),
            jax.ShapeDtypeStruct((rows // seq, HEADS, seq, QK_PAD), _BF),
            jax.ShapeDtypeStruct((rows // seq, HEADS * HEAD_V, seq), _BF),
        ],
        compiler_params=pltpu.CompilerParams(
            dimension_semantics=("parallel",), vmem_limit_bytes=VMEM_LIMIT),
        name="mla_prep",
    )(proj, proj, proj, qnw, kvnw, wq, wk, wvt, *tabs)


def _attn_kernel(q_ref, k_ref, vt_ref, km_ref, vmt_ref, zs_ref, o_ref, m_ref, acc_ref,
                 sa_ref, sb_ref, ma_ref, mb_ref, *, seq, tq, tk):
    def ones_row(width):
        first = lax.broadcasted_iota(jnp.int32, (ONES_ROWS, width), 0) == 0
        return jnp.where(first, 1.0, 0.0).astype(_BF)

    def update(s, smax, vt, c0):
        m_old = m_ref[:, c0:]
        m_new = jnp.maximum(m_old, smax)
        alpha = jnp.exp2(m_old - m_new)
        p = jnp.exp2(s - m_new).astype(_BF)
        if s.shape[0] < vt.shape[1]:
            p = jnp.concatenate(
                [p, jnp.zeros((vt.shape[1] - s.shape[0], s.shape[1]), _BF)], axis=0)
        vt1 = jnp.concatenate([vt, ones_row(vt.shape[1])], axis=0)
        acc_ref[:, c0:] = alpha * acc_ref[:, c0:] + _dot(vt1, p)
        m_ref[:, c0:] = m_new

    causal = (lax.broadcasted_iota(jnp.int32, (tk, tq), 0)
              <= lax.broadcasted_iota(jnp.int32, (tk, tq), 1))
    buf_a, buf_b = (sa_ref, ma_ref), (sb_ref, mb_ref)

    for qi in range(seq // tq):
        rows = slice(qi * tq, (qi + 1) * tq)
        nfull = qi * tq // tk
        nblk = nfull + tq // tk
        m_ref[...] = jnp.full((1, tq), NEG_INF, _F32)
        acc_ref[...] = jnp.zeros((HEAD_V + ONES_ROWS, tq), _F32)

        def diagonal(blk):
            return isinstance(blk, int) and blk >= nfull

        def col0(blk):
            return (blk - nfull) * tk if diagonal(blk) else 0

        def keys(blk):
            if isinstance(blk, int):
                return slice(blk * tk, (blk + 1) * tk)
            return pl.ds(pl.multiple_of(blk * tk, tk), tk)

        def produce(buf, blk):
            s_ref, smax_ref = buf
            c0 = col0(blk)
            s = _dot_nt(k_ref[0, keys(blk), :], q_ref[0, qi * tq + c0:(qi + 1) * tq, :])
            if diagonal(blk):
                s = jnp.where(causal[:, :tq - c0], s, NEG_INF)
            s_ref[:, c0:] = s
            smax_ref[:, c0:] = jnp.max(s, axis=0, keepdims=True)

        def consume(buf, blk):
            s_ref, smax_ref = buf
            c0 = col0(blk)
            update(s_ref[:, c0:], smax_ref[:, c0:], vt_ref[0, :, keys(blk)], c0)

        produce(buf_a, 0)
        s_meta = _dot_nt(km_ref[:N_META, :], q_ref[0, rows, :])
        update(s_meta, jnp.max(s_meta, axis=0, keepdims=True), vmt_ref[...], 0)

        def pair(i, carry):
            produce(buf_b, 2 * i + 1)
            consume(buf_a, 2 * i)
            produce(buf_a, 2 * i + 2)
            consume(buf_b, 2 * i + 1)
            return carry

        trips = (nfull - 1) // 2 if nfull >= 3 else 0
        if trips:
            lax.fori_loop(0, trips, pair, 0)
        done = 2 * trips
        for blk in range(done, nblk - 1):
            produce(buf_b if blk % 2 == 0 else buf_a, blk + 1)
            consume(buf_a if blk % 2 == 0 else buf_b, blk)
        consume(buf_a if (nblk - 1) % 2 == 0 else buf_b, nblk - 1)

        y = acc_ref[:HEAD_V, :] * (1.0 / acc_ref[HEAD_V:HEAD_V + 1, :])
        o_ref[0, rows, :] = (y.T * zs_ref[0, rows, :].astype(_F32)).astype(_BF)


def _attention(q, k, vt, k_meta, vt_meta, proj3, *, tq, tk):
    batch, seq = q.shape[0], q.shape[2]
    return pl.pallas_call(
        functools.partial(_attn_kernel, seq=seq, tq=tq, tk=tk),
        grid=(batch, HEADS),
        in_specs=[
            pl.BlockSpec((None, 1, seq, QK_PAD), lambda b, h: (b, h, 0, 0)),
            pl.BlockSpec((None, 1, seq, QK_PAD), lambda b, h: (b, h, 0, 0)),
            pl.BlockSpec((1, HEAD_V, seq), lambda b, h: (b, h, 0)),
            pl.BlockSpec((None, CHUNK, QK_PAD), lambda b, h: (h, 0, 0)),
            pl.BlockSpec((HEAD_V, CHUNK), lambda b, h: (h, 0)),
            pl.BlockSpec((1, seq, HEAD_V), lambda b, h: (b, 0, ZM_OFF // HEAD_V + h)),
        ],
        out_specs=pl.BlockSpec((1, seq, HEAD_V), lambda b, h: (b, 0, h)),
        out_shape=jax.ShapeDtypeStruct((batch, seq, HEADS * HEAD_V), _BF),
        scratch_shapes=[pltpu.VMEM((1, tq), _F32),
                        pltpu.VMEM((HEAD_V + ONES_ROWS, tq), _F32),
                        pltpu.VMEM((tk, tq), _F32), pltpu.VMEM((tk, tq), _F32),
                        pltpu.VMEM((1, tq), _F32), pltpu.VMEM((1, tq), _F32)],
        compiler_params=pltpu.CompilerParams(
            dimension_semantics=("parallel", "parallel"), vmem_limit_bytes=VMEM_LIMIT),
        name="attention",
    )(q, k, vt, k_meta, vt_meta, proj3)


def _rope128(t, c, s):
    return t * c + pltpu.roll(t, 64, 1) * s


def _ret_kernel(rq_ref, rk_ref, rv_ref, zs_ref, rkm_ref, rvm_ref, c_ref, s_ref, cm_ref, sm_ref,
                dec_ref, zeta_ref, xi_ref, gch_ref, gw_ref, gb_ref, o_ref, *, seq, group):
    kscale = RET_D ** -0.5
    dec = dec_ref[0]
    zeta = zeta_ref[0]
    xi = xi_ref[0]
    gch = gch_ref[0]
    gw = gw_ref[...]
    gb = gb_ref[...]

    km = _rope128(rkm_ref[...].astype(_F32), cm_ref[...], sm_ref[...]) * kscale
    state0 = _dot_tn((km * pltpu.roll(zeta, N_META, 0)).astype(_BF), rvm_ref[...])

    rows = group * CHUNK
    bdims = ((0,), (0,))

    def body(gi, state):
        rs = pl.multiple_of(gi * rows, rows)
        c = c_ref[pl.ds(rs, rows), :]
        s = s_ref[pl.ds(rs, rows), :]
        q = _rope128(rq_ref[0, pl.ds(rs, rows), :].astype(_F32), c, s)
        k = _rope128(rk_ref[0, pl.ds(rs, rows), :].astype(_F32), c, s) * kscale
        q3 = q.reshape(group, CHUNK, RET_D)
        k3 = k.reshape(group, CHUNK, RET_D)
        v3 = rv_ref[0, pl.ds(rs, rows), :].reshape(group, CHUNK, RET_D)
        kz = jnp.swapaxes((k3 * zeta[None]).astype(_BF), 1, 2)
        kv = lax.dot_general(kz, v3, (((2,), (1,)), bdims), preferred_element_type=_F32)
        qb = q3.astype(_BF)
        sc = lax.dot_general(qb, k3.astype(_BF), (((2,), (2,)), bdims),
                             preferred_element_type=_F32) * dec[None]
        states = []
        for ci in range(group):
            states.append(state.astype(_BF))
            state = state * gch + kv[ci]
        o = (lax.dot_general(sc.astype(_BF), v3, (((2,), (1,)), bdims),
                             preferred_element_type=_F32)
             + lax.dot_general(qb, jnp.stack(states), (((2,), (1,)), bdims),
                               preferred_element_type=_F32) * xi[None])
        o = o.reshape(rows, RET_D)
        mu = jnp.mean(o, axis=-1, keepdims=True)
        d = o - mu
        var = jnp.mean(d * d, axis=-1, keepdims=True)
        y = d * lax.rsqrt(var + GN_EPS) * gw + gb
        o_ref[0, pl.ds(rs, rows), :] = (
            y * zs_ref[0, pl.ds(rs, rows), :].astype(_F32)).astype(_BF)
        return state

    lax.fori_loop(0, seq // rows, body, state0)


def _retention(proj3, proj_meta, tabs_x, tabs_m, consts, gn_w, gn_b, *, group):
    batch, seq = proj3.shape[0], proj3.shape[1]
    col = lambda off: (lambda b, h: (b, 0, off // RET_D + h))
    mcol = lambda off: (lambda b, h: (0, off // RET_D + h))
    head3 = pl.BlockSpec((1, CHUNK, RET_D), lambda b, h: (h, 0, 0))
    return pl.pallas_call(
        functools.partial(_ret_kernel, seq=seq, group=group),
        grid=(batch, HEADS),
        in_specs=[
            pl.BlockSpec((1, seq, RET_D), col(RQ_OFF)),
            pl.BlockSpec((1, seq, RET_D), col(RK_OFF)),
            pl.BlockSpec((1, seq, RET_D), col(RV_OFF)),
            pl.BlockSpec((1, seq, RET_D), col(ZR_OFF)),
            pl.BlockSpec((CHUNK, RET_D), mcol(RK_OFF)),
            pl.BlockSpec((CHUNK, RET_D), mcol(RV_OFF)),
            pl.BlockSpec((seq, RET_D), lambda b, h: (0, 0)),
            pl.BlockSpec((seq, RET_D), lambda b, h: (0, 0)),
            pl.BlockSpec((CHUNK, RET_D), lambda b, h: (0, 0)),
            pl.BlockSpec((CHUNK, RET_D), lambda b, h: (0, 0)),
            head3, head3, head3, head3,
            pl.BlockSpec((1, RET_D), lambda b, h: (0, h)),
            pl.BlockSpec((1, RET_D), lambda b, h: (0, h)),
        ],
        out_specs=pl.BlockSpec((1, seq, RET_D), lambda b, h: (b, 0, h)),
        out_shape=jax.ShapeDtypeStruct((batch, seq, HEADS * RET_D), _BF),
        compiler_params=pltpu.CompilerParams(
            dimension_semantics=("parallel", "parallel"), vmem_limit_bytes=VMEM_LIMIT),
        name="retention",
    )(proj3, proj3, proj3, proj3, proj_meta, proj_meta, *tabs_x, *tabs_m, *consts, gn_w, gn_b)


def _out_proj_kernel(ym_ref, yr_ref, g0_ref, g1_ref, x_ref, wbm_ref, wbr_ref, wo_ref, fw_ref, o_ref):
    m1 = _dot(ym_ref[...], wbm_ref[...])
    m2 = _dot(yr_ref[...], wbr_ref[...])
    merged = g0_ref[...].astype(_F32) * m1 + g1_ref[...].astype(_F32) * m2
    h = x_ref[...] + _dot(merged.astype(_BF), wo_ref[...])
    ms = jnp.mean(h * h, axis=-1, keepdims=True)
    o_ref[...] = h * lax.rsqrt(ms + NORM_EPS) * fw_ref[...]


def _out_proj(y_mla, y_ret, proj, x2d, wbm, wbr, wo, fw, *, tm):
    rows = x2d.shape[0]
    width = HEADS * HEAD_V
    const = lambda shape: pl.BlockSpec(shape, lambda i: (0, 0), pipeline_mode=pl.Buffered(1))
    return pl.pallas_call(
        _out_proj_kernel,
        grid=(rows // tm,),
        in_specs=[
            pl.BlockSpec((tm, width), lambda i: (i, 0)),
            pl.BlockSpec((tm, width), lambda i: (i, 0)),
            pl.BlockSpec((tm, D_MODEL), lambda i: (i, GATE_OFF // D_MODEL)),
            pl.BlockSpec((tm, D_MODEL), lambda i: (i, GATE_OFF // D_MODEL + 1)),
            pl.BlockSpec((tm, D_MODEL), lambda i: (i, 0)),
            const((width, D_MODEL)), const((width, D_MODEL)), const((D_MODEL, D_MODEL)),
            const((1, D_MODEL)),
        ],
        out_specs=pl.BlockSpec((tm, D_MODEL), lambda i: (i, 0)),
        out_shape=jax.ShapeDtypeStruct((rows, D_MODEL), _F32),
        compiler_params=pltpu.CompilerParams(
            dimension_semantics=("parallel",), vmem_limit_bytes=VMEM_LIMIT),
        name="out_proj",
    )(y_mla, y_ret, proj, proj, x2d, wbm, wbr, wo, fw)


def _rope_tables(seq):
    pos = np.arange(CHUNK + seq, dtype=np.float64) - META_PAD
    z32 = np.zeros((pos.shape[0], 32))
    inv64 = ROPE_BASE ** (-np.arange(0, ROPE, 2, dtype=np.float64) / ROPE)
    ang = pos[:, None] * inv64[None, :]
    cos, sin = np.cos(ang), np.sin(ang)
    c64 = np.concatenate([cos, cos, z32, z32], axis=1)
    s1 = np.concatenate([-sin, z32, z32, z32], axis=1)
    s2 = np.concatenate([z32, sin, z32, z32], axis=1)
    inv128 = ROPE_BASE ** (-np.arange(0, RET_D, 2, dtype=np.float64) / RET_D)
    ang = pos[:, None] * inv128[None, :]
    cos, sin = np.cos(ang), np.sin(ang)
    c128 = np.concatenate([cos, cos], axis=1)
    s128 = np.concatenate([-sin, sin], axis=1)
    return tuple(t.astype(np.float32) for t in (c64, s1, s2, c128, s128))


def _decay_tables():
    log_g = jnp.log1p(-(2.0 ** (-5.0 - jnp.arange(HEADS, dtype=_F32))))
    n = jnp.arange(CHUNK, dtype=_F32)
    diff = n[:, None] - n[None, :]
    dec = jnp.where(diff >= 0, jnp.exp(log_g[:, None, None] * jnp.maximum(diff, 0.0)), 0.0)
    ones = jnp.ones((1, 1, RET_D), _F32)
    zeta = jnp.exp(log_g[:, None] * (CHUNK - 1.0 - n))[:, :, None] * ones
    xi = jnp.exp(log_g[:, None] * (n + 1.0))[:, :, None] * ones
    gch = jnp.exp(log_g * CHUNK)[:, None, None] * jnp.ones((1, CHUNK, RET_D), _F32)
    return dec, zeta, xi, gch


def _pack_w_in_kernel(w_ref, o_ref, tail_ref, *, tr):
    j = pl.program_id(0)
    pad = ZM_OFF - KPE_END

    @pl.when(j == 0)
    def _():
        o_ref[:KPE_END, :] = w_ref[:KPE_END, :].astype(_BF)
        o_ref[KPE_END:, :] = jnp.zeros((tr - KPE_END, D_MODEL), _BF)

    @pl.when(j > 0)
    def _():
        o_ref[:pad, :] = tail_ref[...]
        o_ref[pad:, :] = w_ref[:tr - pad, :].astype(_BF)

    @pl.when(j + 1 < pl.num_programs(0))
    def _():
        tail_ref[...] = w_ref[tr - pad:, :].astype(_BF)


def _pack_w_in(w_in_t, *, tr):
    assert tr == ZM_OFF and IN_PAD % tr == 0
    steps = IN_PAD // tr
    width = w_in_t.shape[1]
    assert (steps - 1) * tr + tr - (ZM_OFF - KPE_END) == width
    return pl.pallas_call(
        functools.partial(_pack_w_in_kernel, tr=tr),
        grid=(steps,),
        in_specs=[pl.BlockSpec((None, tr, D_MODEL), lambda j: (0, j, 0))],
        out_specs=pl.BlockSpec((tr, D_MODEL), lambda j: (j, 0)),
        out_shape=jax.ShapeDtypeStruct((IN_PAD, D_MODEL), _BF),
        scratch_shapes=[pltpu.VMEM((ZM_OFF - KPE_END, D_MODEL), _BF)],
        compiler_params=pltpu.CompilerParams(
            dimension_semantics=("arbitrary",), vmem_limit_bytes=VMEM_LIMIT),
        name="pack_w_in",
    )(w_in_t)


def _pack_w_uq(w_uq):
    w = w_uq.reshape(Q_RANK, HEADS, NOPE + ROPE)
    w = jnp.pad(w, ((0, 0), (0, 0), (0, QK_PAD - NOPE - ROPE)))
    return w.reshape(Q_RANK, HEADS * QK_PAD).astype(_BF)


def _split_w_ukv(w_ukv):
    w = w_ukv.reshape(KV_RANK, HEADS, NOPE + HEAD_V)
    wk = w[:, :, :NOPE].reshape(KV_RANK, HEADS * NOPE)
    wvt = w[:, :, NOPE:].reshape(KV_RANK, HEADS * HEAD_V).T
    return wk.astype(_BF), wvt.astype(_BF)


def kernel(x, meta, norm_w, w_in, mla_q_norm_w, mla_w_uq, mla_kv_norm_w, mla_w_ukv, ret_gn_w,
           ret_gn_b, w_branch_mla, w_branch_ret, w_out, final_norm_w):
    batch, seq, _ = x.shape
    rows = batch * seq
    x2d = x.reshape(rows, D_MODEL)
    meta_chunk = jnp.pad(meta.astype(x.dtype), ((0, META_PAD), (0, 0)))

    w_in_bf = _pack_w_in(jnp.swapaxes(w_in, 1, 2), tr=ZM_OFF)
    wq = _pack_w_uq(mla_w_uq[0])
    wk, wvt = _split_w_ukv(mla_w_ukv[0])
    qnw, kvnw = mla_q_norm_w, mla_kv_norm_w
    scale = float((NOPE + ROPE) ** -0.5 * np.log2(np.e))

    c64, s1, s2, c128, s128 = _rope_tables(seq)
    meta_rows = slice(META_PAD, META_PAD + CHUNK)
    tabs64_x = (c64[CHUNK:], s1[CHUNK:], s2[CHUNK:])
    tabs64_m = (c64[meta_rows], s1[meta_rows], s2[meta_rows])
    tabs128_x = (c128[CHUNK:], s128[CHUNK:])
    tabs128_m = (c128[meta_rows], s128[meta_rows])

    proj = _in_proj(x2d, norm_w, w_in_bf, tm=1024, tn=1024)
    proj_m = _in_proj(meta_chunk, norm_w, w_in_bf, tm=CHUNK, tn=1024)

    q, k, vt = _mla_prep(proj, qnw, kvnw, wq, wk, wvt, tabs64_x, tm=1024, scale=scale)
    _, k_m, vt_m = _mla_prep(proj_m, qnw, kvnw, wq, wk, wvt, tabs64_m, tm=CHUNK, scale=scale)

    proj3 = proj.reshape(batch, seq, IN_PAD)
    y_mla = _attention(q, k, vt, k_m[0], vt_m[0], proj3, tq=1024, tk=512)
    y_ret = _retention(proj3, proj_m, tabs128_x, tabs128_m, _decay_tables(),
                       ret_gn_w, ret_gn_b, group=32)

    out = _out_proj(y_mla.reshape(rows, -1), y_ret.reshape(rows, -1), proj, x2d,
                    w_branch_mla[0].astype(_BF), w_branch_ret[0].astype(_BF),
                    w_out[0].astype(_BF), final_norm_w.reshape(1, D_MODEL), tm=256)
    return out.reshape(batch, seq, D_MODEL)
```

```python
import functools

import numpy as np
import jax
import jax.numpy as jnp
from jax import lax
from jax.experimental import pallas as pl
from jax.experimental.pallas import tpu as pltpu

D_MODEL = 2048
N_META = 16
CHUNK = 128
META_PAD = CHUNK - N_META
HEADS = 8
NOPE = 128
ROPE = 64
HEAD_V = 128
Q_RANK = 512
KV_RANK = 256
QK_PAD = 256
ONES_ROWS = 16
RET_D = 128
ROPE_BASE = 10000.0
NORM_EPS = 1e-6
GN_EPS = 1e-5
NEG_INF = -1e30

CQ_OFF = 0
CKV_OFF = 512
KPE_OFF = 768
ZM_OFF = 1024
RQ_OFF = 2048
RK_OFF = 3072
RV_OFF = 4096
ZR_OFF = 5120
GATE_OFF = 6144
IN_PAD = 10240
KPE_END = KPE_OFF + ROPE

V7X_VMEM_BYTES = 64 * 1024 * 1024
VMEM_LIMIT = V7X_VMEM_BYTES - 8 * 1024 * 1024

IN_PROJ_TM = 1024
IN_PROJ_TN = 1024
MLA_PREP_TM = 1024
ATTN_TQ = 1024
ATTN_TK = 512
RET_GROUP = 32
OUT_PROJ_TM = 256

_BF = jnp.bfloat16
_F32 = jnp.float32


def _dot(a, b):
    return jnp.dot(a, b, preferred_element_type=_F32)


def _dot_nt(a, b):
    return lax.dot_general(a, b, (((1,), (1,)), ((), ())), preferred_element_type=_F32)


def _dot_tn(a, b):
    return lax.dot_general(a, b, (((0,), (0,)), ((), ())), preferred_element_type=_F32)


def _sigmoid(x):
    return 0.5 * jnp.tanh(0.5 * x) + 0.5


def _in_proj_kernel(x_ref, nw_ref, w_ref, o_ref, xn_ref, *, tn):
    j = pl.program_id(1)

    @pl.when(j == 0)
    def _():
        x = x_ref[...]
        ms = jnp.mean(x * x, axis=-1, keepdims=True)
        xn_ref[...] = (x * lax.rsqrt(ms + NORM_EPS) * nw_ref[...]).astype(_BF)

    acc = _dot_nt(xn_ref[...], w_ref[...])
    col = j * tn
    is_gate = col >= GATE_OFF
    is_z = ((col >= ZM_OFF) & (col < RQ_OFF)) | ((col >= ZR_OFF) & (col < GATE_OFF))

    @pl.when(is_gate)
    def _():
        o_ref[...] = _sigmoid(acc).astype(_BF)

    @pl.when(is_z)
    def _():
        o_ref[...] = (acc * _sigmoid(acc)).astype(_BF)

    @pl.when(jnp.logical_not(is_gate | is_z))
    def _():
        o_ref[...] = acc.astype(_BF)


def _in_proj(x2d, norm_w, w_in_bf, cols, *, tm, tn):
    rows = x2d.shape[0]
    assert all(off % tn == 0 for off in (ZM_OFF, RQ_OFF, ZR_OFF, GATE_OFF, IN_PAD, cols))
    return pl.pallas_call(
        functools.partial(_in_proj_kernel, tn=tn),
        grid=(rows // tm, cols // tn),
        in_specs=[
            pl.BlockSpec((tm, D_MODEL), lambda i, j: (i, 0)),
            pl.BlockSpec((1, D_MODEL), lambda i, j: (0, 0)),
            pl.BlockSpec((tn, D_MODEL), lambda i, j: (j, 0)),
        ],
        out_specs=pl.BlockSpec((tm, tn), lambda i, j: (i, j)),
        out_shape=jax.ShapeDtypeStruct((rows, cols), _BF),
        scratch_shapes=[pltpu.VMEM((tm, D_MODEL), _BF)],
        compiler_params=pltpu.CompilerParams(
            dimension_semantics=("parallel", "arbitrary"), vmem_limit_bytes=VMEM_LIMIT),
        name="in_proj",
    )(x2d, norm_w, w_in_bf)


def _rms(x, w):
    ms = jnp.mean(x * x, axis=-1, keepdims=True)
    return x * lax.rsqrt(ms + NORM_EPS) * w


def _rope64(t, c, s1, s2):
    return t * c + pltpu.roll(t, 96, 1) * s1 + pltpu.roll(t, 32, 1) * s2


def _mla_prep_kernel(cq_ref, ckv_ref, kpe_ref, qnw_ref, kvnw_ref, wq_ref, wk_ref, wvt_ref,
                     c_ref, s1_ref, s2_ref, q_ref, k_ref, vt_ref, *, scale):
    c, s1, s2 = c_ref[...], s1_ref[...], s2_ref[...]

    cqn = _rms(cq_ref[...].astype(_F32), qnw_ref[...]).astype(_BF)
    q = _dot(cqn, wq_ref[...])
    for h in range(HEADS):
        lo = h * QK_PAD
        q_ref[0, h, :, :NOPE] = (q[:, lo:lo + NOPE] * scale).astype(_BF)
        q_ref[0, h, :, NOPE:] = (
            _rope64(q[:, lo + NOPE:lo + QK_PAD], c, s1, s2) * scale).astype(_BF)

    ckvn = _rms(ckv_ref[...].astype(_F32), kvnw_ref[...]).astype(_BF)
    kn = _dot(ckvn, wk_ref[...])
    kpe = _rope64(kpe_ref[...].astype(_F32), c, s1, s2).astype(_BF)
    for h in range(HEADS):
        k_ref[0, h, :, :NOPE] = kn[:, h * NOPE:(h + 1) * NOPE].astype(_BF)
        k_ref[0, h, :, NOPE:] = kpe
    vt_ref[0] = _dot_nt(wvt_ref[...], ckvn).astype(_BF)


def _mla_prep(proj, qnw, kvnw, wq, wk, wvt, tabs, *, tm, scale):
    rows = proj.shape[0]
    seq = tabs[0].shape[0]
    tab_blocks = seq // tm
    full = lambda shape: pl.BlockSpec(shape, lambda i: (0, 0))
    tab_spec = pl.BlockSpec((tm, 128), lambda i: (i % tab_blocks, 0))
    return pl.pallas_call(
        functools.partial(_mla_prep_kernel, scale=scale),
        grid=(rows // tm,),
        in_specs=[
            pl.BlockSpec((tm, Q_RANK), lambda i: (i, CQ_OFF // Q_RANK)),
            pl.BlockSpec((tm, KV_RANK), lambda i: (i, CKV_OFF // KV_RANK)),
            pl.BlockSpec((tm, 128), lambda i: (i, KPE_OFF // 128)),
            full((1, Q_RANK)), full((1, KV_RANK)),
            full((Q_RANK, HEADS * QK_PAD)), full((KV_RANK, HEADS * NOPE)),
            full((HEADS * HEAD_V, KV_RANK)),
            tab_spec, tab_spec, tab_spec,
        ],
        out_specs=[
            pl.BlockSpec((1, HEADS, tm, QK_PAD), lambda i: (i // tab_blocks, 0, i % tab_blocks, 0)),
            pl.BlockSpec((1, HEADS, tm, QK_PAD), lambda i: (i // tab_blocks, 0, i % tab_blocks, 0)),
            pl.BlockSpec((1, HEADS * HEAD_V, tm), lambda i: (i // tab_blocks, 0, i % tab_blocks)),
        ],
        out_shape=[
            jax.ShapeDtypeStruct((rows // seq, HEADS, seq, QK_PAD), _BF),
            jax.ShapeDtypeStruct((rows // seq, HEADS, seq, QK_PAD), _BF),
            jax.ShapeDtypeStruct((rows // seq, HEADS * HEAD_V, seq), _BF),
        ],
        compiler_params=pltpu.CompilerParams(
            dimension_semantics=("parallel",), vmem_limit_bytes=VMEM_LIMIT),
        name="mla_prep",
    )(proj, proj, proj, qnw, kvnw, wq, wk, wvt, *tabs)


def _attn_kernel(q_ref, k_ref, vt_ref, km_ref, vmt_ref, zs_ref, o_ref, m_ref, acc_ref,
                 sa_ref, sb_ref, ma_ref, mb_ref, *, seq, tq, tk):
    def ones_row(width):
        first = lax.broadcasted_iota(jnp.int32, (ONES_ROWS, width), 0) == 0
        return jnp.where(first, 1.0, 0.0).astype(_BF)

    def update(s, smax, vt, c0):
        m_old = m_ref[:, c0:]
        m_new = jnp.maximum(m_old, smax)
        alpha = jnp.exp2(m_old - m_new)
        p = jnp.exp2(s - m_new).astype(_BF)
        if s.shape[0] < vt.shape[1]:
            p = jnp.concatenate(
                [p, jnp.zeros((vt.shape[1] - s.shape[0], s.shape[1]), _BF)], axis=0)
        vt1 = jnp.concatenate([vt, ones_row(vt.shape[1])], axis=0)
        acc_ref[:, c0:] = alpha * acc_ref[:, c0:] + _dot(vt1, p)
        m_ref[:, c0:] = m_new

    causal = (lax.broadcasted_iota(jnp.int32, (tk, tq), 0)
              <= lax.broadcasted_iota(jnp.int32, (tk, tq), 1))
    buf_a, buf_b = (sa_ref, ma_ref), (sb_ref, mb_ref)

    for qi in range(seq // tq):
        rows = slice(qi * tq, (qi + 1) * tq)
        nfull = qi * tq // tk
        nblk = nfull + tq // tk
        m_ref[...] = jnp.full((1, tq), NEG_INF, _F32)
        acc_ref[...] = jnp.zeros((HEAD_V + ONES_ROWS, tq), _F32)

        def diagonal(blk):
            return isinstance(blk, int) and blk >= nfull

        def col0(blk):
            return (blk - nfull) * tk if diagonal(blk) else 0

        def keys(blk):
            if isinstance(blk, int):
                return slice(blk * tk, (blk + 1) * tk)
            return pl.ds(pl.multiple_of(blk * tk, tk), tk)

        def produce(buf, blk):
            s_ref, smax_ref = buf
            c0 = col0(blk)
            s = _dot_nt(k_ref[0, keys(blk), :], q_ref[0, qi * tq + c0:(qi + 1) * tq, :])
            if diagonal(blk):
                s = jnp.where(causal[:, :tq - c0], s, NEG_INF)
            s_ref[:, c0:] = s
            smax_ref[:, c0:] = jnp.max(s, axis=0, keepdims=True)

        def consume(buf, blk):
            s_ref, smax_ref = buf
            c0 = col0(blk)
            update(s_ref[:, c0:], smax_ref[:, c0:], vt_ref[0, :, keys(blk)], c0)

        produce(buf_a, 0)
        s_meta = _dot_nt(km_ref[:N_META, :], q_ref[0, rows, :])
        update(s_meta, jnp.max(s_meta, axis=0, keepdims=True), vmt_ref[...], 0)

        def pair(i, carry):
            produce(buf_b, 2 * i + 1)
            consume(buf_a, 2 * i)
            produce(buf_a, 2 * i + 2)
            consume(buf_b, 2 * i + 1)
            return carry

        trips = (nfull - 1) // 2 if nfull >= 3 else 0
        if trips:
            lax.fori_loop(0, trips, pair, 0)
        done = 2 * trips
        for blk in range(done, nblk - 1):
            produce(buf_b if blk % 2 == 0 else buf_a, blk + 1)
            consume(buf_a if blk % 2 == 0 else buf_b, blk)
        consume(buf_a if (nblk - 1) % 2 == 0 else buf_b, nblk - 1)

        y = acc_ref[:HEAD_V, :] * (1.0 / acc_ref[HEAD_V:HEAD_V + 1, :])
        o_ref[0, rows, :] = (y.T * zs_ref[0, rows, :].astype(_F32)).astype(_BF)


def _attention(q, k, vt, k_meta, vt_meta, proj3, *, tq, tk):
    batch, seq = q.shape[0], q.shape[2]
    return pl.pallas_call(
        functools.partial(_attn_kernel, seq=seq, tq=tq, tk=tk),
        grid=(batch, HEADS),
        in_specs=[
            pl.BlockSpec((None, 1, seq, QK_PAD), lambda b, h: (b, h, 0, 0)),
            pl.BlockSpec((None, 1, seq, QK_PAD), lambda b, h: (b, h, 0, 0)),
            pl.BlockSpec((1, HEAD_V, seq), lambda b, h: (b, h, 0)),
            pl.BlockSpec((None, CHUNK, QK_PAD), lambda b, h: (h, 0, 0)),
            pl.BlockSpec((HEAD_V, CHUNK), lambda b, h: (h, 0)),
            pl.BlockSpec((1, seq, HEAD_V), lambda b, h: (b, 0, ZM_OFF // HEAD_V + h)),
        ],
        out_specs=pl.BlockSpec((1, seq, HEAD_V), lambda b, h: (b, 0, h)),
        out_shape=jax.ShapeDtypeStruct((batch, seq, HEADS * HEAD_V), _BF),
        scratch_shapes=[pltpu.VMEM((1, tq), _F32),
                        pltpu.VMEM((HEAD_V + ONES_ROWS, tq), _F32),
                        pltpu.VMEM((tk, tq), _F32), pltpu.VMEM((tk, tq), _F32),
                        pltpu.VMEM((1, tq), _F32), pltpu.VMEM((1, tq), _F32)],
        compiler_params=pltpu.CompilerParams(
            dimension_semantics=("parallel", "parallel"), vmem_limit_bytes=VMEM_LIMIT),
        name="attention",
    )(q, k, vt, k_meta, vt_meta, proj3)


def _rope128(t, c, s):
    return t * c + pltpu.roll(t, 64, 1) * s


def _ret_kernel(rq_ref, rk_ref, rv_ref, zs_ref, rkm_ref, rvm_ref, c_ref, s_ref, cm_ref, sm_ref,
                dec_ref, zeta_ref, xi_ref, gch_ref, gw_ref, gb_ref, o_ref, *, seq, group):
    kscale = RET_D ** -0.5
    dec = dec_ref[0]
    zeta = zeta_ref[0]
    xi = xi_ref[0]
    gch = gch_ref[0]
    gw = gw_ref[...]
    gb = gb_ref[...]

    km = _rope128(rkm_ref[...].astype(_F32), cm_ref[...], sm_ref[...]) * kscale
    state0 = _dot_tn((km * pltpu.roll(zeta, N_META, 0)).astype(_BF), rvm_ref[...])

    rows = group * CHUNK
    bdims = ((0,), (0,))

    def body(gi, state):
        rs = pl.multiple_of(gi * rows, rows)
        c = c_ref[pl.ds(rs, rows), :]
        s = s_ref[pl.ds(rs, rows), :]
        q = _rope128(rq_ref[0, pl.ds(rs, rows), :].astype(_F32), c, s)
        k = _rope128(rk_ref[0, pl.ds(rs, rows), :].astype(_F32), c, s) * kscale
        q3 = q.reshape(group, CHUNK, RET_D)
        k3 = k.reshape(group, CHUNK, RET_D)
        v3 = rv_ref[0, pl.ds(rs, rows), :].reshape(group, CHUNK, RET_D)
        kz = jnp.swapaxes((k3 * zeta[None]).astype(_BF), 1, 2)
        kv = lax.dot_general(kz, v3, (((2,), (1,)), bdims), preferred_element_type=_F32)
        qb = q3.astype(_BF)
        sc = lax.dot_general(qb, k3.astype(_BF), (((2,), (2,)), bdims),
                             preferred_element_type=_F32) * dec[None]
        states = []
        for ci in range(group):
            states.append(state.astype(_BF))
            state = state * gch + kv[ci]
        o = (lax.dot_general(sc.astype(_BF), v3, (((2,), (1,)), bdims),
                             preferred_element_type=_F32)
             + lax.dot_general(qb, jnp.stack(states), (((2,), (1,)), bdims),
                               preferred_element_type=_F32) * xi[None])
        o = o.reshape(rows, RET_D)
        mu = jnp.mean(o, axis=-1, keepdims=True)
        d = o - mu
        var = jnp.mean(d * d, axis=-1, keepdims=True)
        y = d * lax.rsqrt(var + GN_EPS) * gw + gb
        o_ref[0, pl.ds(rs, rows), :] = (
            y * zs_ref[0, pl.ds(rs, rows), :].astype(_F32)).astype(_BF)
        return state

    lax.fori_loop(0, seq // rows, body, state0)


def _retention(proj3, proj_meta, tabs_x, tabs_m, consts, gn_w, gn_b, *, group):
    batch, seq = proj3.shape[0], proj3.shape[1]
    col = lambda off: (lambda b, h: (b, 0, off // RET_D + h))
    mcol = lambda off: (lambda b, h: (0, off // RET_D + h))
    head3 = pl.BlockSpec((1, CHUNK, RET_D), lambda b, h: (h, 0, 0))
    return pl.pallas_call(
        functools.partial(_ret_kernel, seq=seq, group=group),
        grid=(batch, HEADS),
        in_specs=[
            pl.BlockSpec((1, seq, RET_D), col(RQ_OFF)),
            pl.BlockSpec((1, seq, RET_D), col(RK_OFF)),
            pl.BlockSpec((1, seq, RET_D), col(RV_OFF)),
            pl.BlockSpec((1, seq, RET_D), col(ZR_OFF)),
            pl.BlockSpec((CHUNK, RET_D), mcol(RK_OFF)),
            pl.BlockSpec((CHUNK, RET_D), mcol(RV_OFF)),
            pl.BlockSpec((seq, RET_D), lambda b, h: (0, 0)),
            pl.BlockSpec((seq, RET_D), lambda b, h: (0, 0)),
            pl.BlockSpec((CHUNK, RET_D), lambda b, h: (0, 0)),
            pl.BlockSpec((CHUNK, RET_D), lambda b, h: (0, 0)),
            head3, head3, head3, head3,
            pl.BlockSpec((1, RET_D), lambda b, h: (0, h)),
            pl.BlockSpec((1, RET_D), lambda b, h: (0, h)),
        ],
        out_specs=pl.BlockSpec((1, seq, RET_D), lambda b, h: (b, 0, h)),
        out_shape=jax.ShapeDtypeStruct((batch, seq, HEADS * RET_D), _BF),
        compiler_params=pltpu.CompilerParams(
            dimension_semantics=("parallel", "parallel"), vmem_limit_bytes=VMEM_LIMIT),
        name="retention",
    )(proj3, proj3, proj3, proj3, proj_meta, proj_meta, *tabs_x, *tabs_m, *consts, gn_w, gn_b)


def _out_proj_kernel(ym_ref, yr_ref, g0_ref, g1_ref, x_ref, wbm_ref, wbr_ref, wo_ref, fw_ref, o_ref):
    m1 = _dot(ym_ref[...], wbm_ref[...])
    m2 = _dot(yr_ref[...], wbr_ref[...])
    merged = g0_ref[...].astype(_F32) * m1 + g1_ref[...].astype(_F32) * m2
    h = x_ref[...] + _dot(merged.astype(_BF), wo_ref[...])
    ms = jnp.mean(h * h, axis=-1, keepdims=True)
    o_ref[...] = h * lax.rsqrt(ms + NORM_EPS) * fw_ref[...]


def _out_proj(y_mla, y_ret, proj, x2d, wbm, wbr, wo, fw, *, tm):
    rows = x2d.shape[0]
    width = HEADS * HEAD_V
    const = lambda shape: pl.BlockSpec(shape, lambda i: (0, 0), pipeline_mode=pl.Buffered(1))
    return pl.pallas_call(
        _out_proj_kernel,
        grid=(rows // tm,),
        in_specs=[
            pl.BlockSpec((tm, width), lambda i: (i, 0)),
            pl.BlockSpec((tm, width), lambda i: (i, 0)),
            pl.BlockSpec((tm, D_MODEL), lambda i: (i, GATE_OFF // D_MODEL)),
            pl.BlockSpec((tm, D_MODEL), lambda i: (i, GATE_OFF // D_MODEL + 1)),
            pl.BlockSpec((tm, D_MODEL), lambda i: (i, 0)),
            const((width, D_MODEL)), const((width, D_MODEL)), const((D_MODEL, D_MODEL)),
            const((1, D_MODEL)),
        ],
        out_specs=pl.BlockSpec((tm, D_MODEL), lambda i: (i, 0)),
        out_shape=jax.ShapeDtypeStruct((rows, D_MODEL), _F32),
        compiler_params=pltpu.CompilerParams(
            dimension_semantics=("parallel",), vmem_limit_bytes=VMEM_LIMIT),
        name="out_proj",
    )(y_mla, y_ret, proj, proj, x2d, wbm, wbr, wo, fw)


def _rope_tables(seq):
    pos = np.arange(CHUNK + seq, dtype=np.float64) - META_PAD
    z32 = np.zeros((pos.shape[0], 32))
    inv64 = ROPE_BASE ** (-np.arange(0, ROPE, 2, dtype=np.float64) / ROPE)
    ang = pos[:, None] * inv64[None, :]
    cos, sin = np.cos(ang), np.sin(ang)
    c64 = np.concatenate([cos, cos, z32, z32], axis=1)
    s1 = np.concatenate([-sin, z32, z32, z32], axis=1)
    s2 = np.concatenate([z32, sin, z32, z32], axis=1)
    inv128 = ROPE_BASE ** (-np.arange(0, RET_D, 2, dtype=np.float64) / RET_D)
    ang = pos[:, None] * inv128[None, :]
    cos, sin = np.cos(ang), np.sin(ang)
    c128 = np.concatenate([cos, cos], axis=1)
    s128 = np.concatenate([-sin, sin], axis=1)
    return tuple(t.astype(np.float32) for t in (c64, s1, s2, c128, s128))


def _decay_tables():
    log_g = jnp.log1p(-(2.0 ** (-5.0 - jnp.arange(HEADS, dtype=_F32))))
    n = jnp.arange(CHUNK, dtype=_F32)
    diff = n[:, None] - n[None, :]
    dec = jnp.where(diff >= 0, jnp.exp(log_g[:, None, None] * jnp.maximum(diff, 0.0)), 0.0)
    ones = jnp.ones((1, 1, RET_D), _F32)
    zeta = jnp.exp(log_g[:, None] * (CHUNK - 1.0 - n))[:, :, None] * ones
    xi = jnp.exp(log_g[:, None] * (n + 1.0))[:, :, None] * ones
    gch = jnp.exp(log_g * CHUNK)[:, None, None] * jnp.ones((1, CHUNK, RET_D), _F32)
    return dec, zeta, xi, gch


def _pack_w_in_kernel(w_ref, o_ref, tail_ref, *, tr):
    j = pl.program_id(0)
    pad = ZM_OFF - KPE_END

    @pl.when(j == 0)
    def _():
        o_ref[:KPE_END, :] = w_ref[:KPE_END, :].astype(_BF)
        o_ref[KPE_END:, :] = jnp.zeros((tr - KPE_END, D_MODEL), _BF)

    @pl.when(j > 0)
    def _():
        o_ref[:pad, :] = tail_ref[...]
        o_ref[pad:, :] = w_ref[:tr - pad, :].astype(_BF)

    @pl.when(j + 1 < pl.num_programs(0))
    def _():
        tail_ref[...] = w_ref[tr - pad:, :].astype(_BF)


def _pack_w_in(w_in_t, *, tr):
    assert tr == ZM_OFF and IN_PAD % tr == 0
    steps = IN_PAD // tr
    width = w_in_t.shape[1]
    assert (steps - 1) * tr + tr - (ZM_OFF - KPE_END) == width
    return pl.pallas_call(
        functools.partial(_pack_w_in_kernel, tr=tr),
        grid=(steps,),
        in_specs=[pl.BlockSpec((None, tr, D_MODEL), lambda j: (0, j, 0))],
        out_specs=pl.BlockSpec((tr, D_MODEL), lambda j: (j, 0)),
        out_shape=jax.ShapeDtypeStruct((IN_PAD, D_MODEL), _BF),
        scratch_shapes=[pltpu.VMEM((ZM_OFF - KPE_END, D_MODEL), _BF)],
        compiler_params=pltpu.CompilerParams(
            dimension_semantics=("arbitrary",), vmem_limit_bytes=VMEM_LIMIT),
        name="pack_w_in",
    )(w_in_t)


def _pack_w_uq(w_uq):
    w = w_uq.reshape(Q_RANK, HEADS, NOPE + ROPE)
    w = jnp.pad(w, ((0, 0), (0, 0), (0, QK_PAD - NOPE - ROPE)))
    return w.reshape(Q_RANK, HEADS * QK_PAD).astype(_BF)


def _split_w_ukv(w_ukv):
    w = w_ukv.reshape(KV_RANK, HEADS, NOPE + HEAD_V)
    wk = w[:, :, :NOPE].reshape(KV_RANK, HEADS * NOPE)
    wvt = w[:, :, NOPE:].reshape(KV_RANK, HEADS * HEAD_V).T
    return wk.astype(_BF), wvt.astype(_BF)


def kernel(x, meta, norm_w, w_in, mla_q_norm_w, mla_w_uq, mla_kv_norm_w, mla_w_ukv, ret_gn_w,
           ret_gn_b, w_branch_mla, w_branch_ret, w_out, final_norm_w):
    batch, seq, _ = x.shape
    rows = batch * seq
    x2d = x.reshape(rows, D_MODEL)
    meta_chunk = jnp.pad(meta.astype(x.dtype), ((0, META_PAD), (0, 0)))

    w_in_bf = _pack_w_in(jnp.swapaxes(w_in, 1, 2), tr=ZM_OFF)
    wq = _pack_w_uq(mla_w_uq[0])
    wk, wvt = _split_w_ukv(mla_w_ukv[0])
    qnw, kvnw = mla_q_norm_w, mla_kv_norm_w
    scale = float((NOPE + ROPE) ** -0.5 * np.log2(np.e))

    c64, s1, s2, c128, s128 = _rope_tables(seq)
    meta_rows = slice(META_PAD, META_PAD + CHUNK)
    tabs64_x = (c64[CHUNK:], s1[CHUNK:], s2[CHUNK:])
    tabs64_m = (c64[meta_rows], s1[meta_rows], s2[meta_rows])
    tabs128_x = (c128[CHUNK:], s128[CHUNK:])
    tabs128_m = (c128[meta_rows], s128[meta_rows])

    proj = _in_proj(x2d, norm_w, w_in_bf, IN_PAD, tm=IN_PROJ_TM, tn=IN_PROJ_TN)
    proj_m = _in_proj(meta_chunk, norm_w, w_in_bf, GATE_OFF, tm=CHUNK, tn=IN_PROJ_TN)

    q, k, vt = _mla_prep(proj, qnw, kvnw, wq, wk, wvt, tabs64_x, tm=MLA_PREP_TM, scale=scale)
    _, k_m, vt_m = _mla_prep(proj_m, qnw, kvnw, wq, wk, wvt, tabs64_m, tm=CHUNK, scale=scale)

    proj3 = proj.reshape(batch, seq, IN_PAD)
    y_mla = _attention(q, k, vt, k_m[0], vt_m[0], proj3, tq=ATTN_TQ, tk=ATTN_TK)
    y_ret = _retention(proj3, proj_m, tabs128_x, tabs128_m, _decay_tables(),
                       ret_gn_w, ret_gn_b, group=RET_GROUP)

    out = _out_proj(y_mla.reshape(rows, -1), y_ret.reshape(rows, -1), proj, x2d,
                    w_branch_mla[0].astype(_BF), w_branch_ret[0].astype(_BF),
                    w_out[0].astype(_BF), final_norm_w.reshape(1, D_MODEL), tm=OUT_PROJ_TM)
    return out.reshape(batch, seq, D_MODEL)
```

```python
import functools

import numpy as np
import jax
import jax.numpy as jnp
from jax import lax
from jax.experimental import pallas as pl
from jax.experimental.pallas import tpu as pltpu

D_MODEL = 2048
N_META = 16
CHUNK = 128
META_PAD = CHUNK - N_META
HEADS = 8
NOPE = 128
ROPE = 64
HEAD_V = 128
Q_RANK = 512
KV_RANK = 256
QK_PAD = 256
ONES_ROWS = 16
RET_D = 128
ROPE_BASE = 10000.0
NORM_EPS = 1e-6
GN_EPS = 1e-5
NEG_INF = -1e30

CQ_OFF = 0
CKV_OFF = 512
KPE_OFF = 768
ZM_OFF = 1024
RQ_OFF = 2048
RK_OFF = 3072
RV_OFF = 4096
ZR_OFF = 5120
GATE_OFF = 6144
IN_PAD = 10240
KPE_END = KPE_OFF + ROPE

V7X_VMEM_BYTES = 64 * 1024 * 1024
VMEM_LIMIT = V7X_VMEM_BYTES - 8 * 1024 * 1024

IN_PROJ_TM = 1024
IN_PROJ_TN = 1024
MLA_PREP_TM = 1024
ATTN_TQ = 1024
ATTN_TK = 512
RET_GROUP = 32
OUT_PROJ_TM = 512

_BF = jnp.bfloat16
_F32 = jnp.float32


def _dot(a, b):
    return jnp.dot(a, b, preferred_element_type=_F32)


def _dot_nt(a, b):
    return lax.dot_general(a, b, (((1,), (1,)), ((), ())), preferred_element_type=_F32)


def _dot_tn(a, b):
    return lax.dot_general(a, b, (((0,), (0,)), ((), ())), preferred_element_type=_F32)


def _sigmoid(x):
    return 0.5 * jnp.tanh(0.5 * x) + 0.5


def _in_proj_kernel(x_ref, nw_ref, w_ref, o_ref, xn_ref, *, tn):
    j = pl.program_id(1)

    @pl.when(j == 0)
    def _():
        x = x_ref[...]
        ms = jnp.mean(x * x, axis=-1, keepdims=True)
        xn_ref[...] = (x * lax.rsqrt(ms + NORM_EPS) * nw_ref[...]).astype(_BF)

    acc = _dot_nt(xn_ref[...], w_ref[...])
    col = j * tn
    is_gate = col >= GATE_OFF
    is_z = ((col >= ZM_OFF) & (col < RQ_OFF)) | ((col >= ZR_OFF) & (col < GATE_OFF))

    @pl.when(is_gate)
    def _():
        o_ref[...] = _sigmoid(acc).astype(_BF)

    @pl.when(is_z)
    def _():
        o_ref[...] = (acc * _sigmoid(acc)).astype(_BF)

    @pl.when(jnp.logical_not(is_gate | is_z))
    def _():
        o_ref[...] = acc.astype(_BF)


def _in_proj(x2d, norm_w, w_in_bf, cols, *, tm, tn):
    rows = x2d.shape[0]
    assert all(off % tn == 0 for off in (ZM_OFF, RQ_OFF, ZR_OFF, GATE_OFF, IN_PAD, cols))
    return pl.pallas_call(
        functools.partial(_in_proj_kernel, tn=tn),
        grid=(rows // tm, cols // tn),
        in_specs=[
            pl.BlockSpec((tm, D_MODEL), lambda i, j: (i, 0)),
            pl.BlockSpec((1, D_MODEL), lambda i, j: (0, 0)),
            pl.BlockSpec((tn, D_MODEL), lambda i, j: (j, 0)),
        ],
        out_specs=pl.BlockSpec((tm, tn), lambda i, j: (i, j)),
        out_shape=jax.ShapeDtypeStruct((rows, cols), _BF),
        scratch_shapes=[pltpu.VMEM((tm, D_MODEL), _BF)],
        compiler_params=pltpu.CompilerParams(
            dimension_semantics=("parallel", "arbitrary"), vmem_limit_bytes=VMEM_LIMIT),
        name="in_proj",
    )(x2d, norm_w, w_in_bf)


def _rms(x, w):
    ms = jnp.mean(x * x, axis=-1, keepdims=True)
    return x * lax.rsqrt(ms + NORM_EPS) * w


def _rope64(t, c, s1, s2):
    return t * c + pltpu.roll(t, 96, 1) * s1 + pltpu.roll(t, 32, 1) * s2


def _mla_prep_kernel(cq_ref, ckv_ref, kpe_ref, qnw_ref, kvnw_ref, wq_ref, wk_ref, wvt_ref,
                     c_ref, s1_ref, s2_ref, q_ref, k_ref, vt_ref, *, scale):
    c, s1, s2 = c_ref[...], s1_ref[...], s2_ref[...]

    cqn = _rms(cq_ref[...].astype(_F32), qnw_ref[...]).astype(_BF)
    q = _dot(cqn, wq_ref[...])
    for h in range(HEADS):
        lo = h * QK_PAD
        q_ref[0, h, :, :NOPE] = (q[:, lo:lo + NOPE] * scale).astype(_BF)
        q_ref[0, h, :, NOPE:] = (
            _rope64(q[:, lo + NOPE:lo + QK_PAD], c, s1, s2) * scale).astype(_BF)

    ckvn = _rms(ckv_ref[...].astype(_F32), kvnw_ref[...]).astype(_BF)
    kn = _dot(ckvn, wk_ref[...])
    kpe = _rope64(kpe_ref[...].astype(_F32), c, s1, s2).astype(_BF)
    for h in range(HEADS):
        k_ref[0, h, :, :NOPE] = kn[:, h * NOPE:(h + 1) * NOPE].astype(_BF)
        k_ref[0, h, :, NOPE:] = kpe
    vt_ref[0] = _dot_nt(wvt_ref[...], ckvn).astype(_BF)


def _mla_prep(proj, qnw, kvnw, wq, wk, wvt, tabs, *, tm, scale):
    rows = proj.shape[0]
    seq = tabs[0].shape[0]
    tab_blocks = seq // tm
    full = lambda shape: pl.BlockSpec(shape, lambda i: (0, 0))
    tab_spec = pl.BlockSpec((tm, 128), lambda i: (i % tab_blocks, 0))
    return pl.pallas_call(
        functools.partial(_mla_prep_kernel, scale=scale),
        grid=(rows // tm,),
        in_specs=[
            pl.BlockSpec((tm, Q_RANK), lambda i: (i, CQ_OFF // Q_RANK)),
            pl.BlockSpec((tm, KV_RANK), lambda i: (i, CKV_OFF // KV_RANK)),
            pl.BlockSpec((tm, 128), lambda i: (i, KPE_OFF // 128)),
            full((1, Q_RANK)), full((1, KV_RANK)),
            full((Q_RANK, HEADS * QK_PAD)), full((KV_RANK, HEADS * NOPE)),
            full((HEADS * HEAD_V, KV_RANK)),
            tab_spec, tab_spec, tab_spec,
        ],
        out_specs=[
            pl.BlockSpec((1, HEADS, tm, QK_PAD), lambda i: (i // tab_blocks, 0, i % tab_blocks, 0)),
            pl.BlockSpec((1, HEADS, tm, QK_PAD), lambda i: (i // tab_blocks, 0, i % tab_blocks, 0)),
            pl.BlockSpec((1, HEADS * HEAD_V, tm), lambda i: (i // tab_blocks, 0, i % tab_blocks)),
        ],
        out_shape=[
            jax.ShapeDtypeStruct((rows // seq, HEADS, seq, QK_PAD), _BF),
            jax.ShapeDtypeStruct((rows // seq, HEADS, seq, QK_PAD), _BF),
            jax.ShapeDtypeStruct((rows // seq, HEADS * HEAD_V, seq), _BF),
        ],
        compiler_params=pltpu.CompilerParams(
            dimension_semantics=("parallel",), vmem_limit_bytes=VMEM_LIMIT),
        name="mla_prep",
    )(proj, proj, proj, qnw, kvnw, wq, wk, wvt, *tabs)


def _attn_kernel(q_ref, k_ref, vt_ref, km_ref, vmt_ref, zs_ref, o_ref, m_ref, acc_ref,
                 sa_ref, sb_ref, ma_ref, mb_ref, *, seq, tq, tk):
    def ones_row(width):
        first = lax.broadcasted_iota(jnp.int32, (ONES_ROWS, width), 0) == 0
        return jnp.where(first, 1.0, 0.0).astype(_BF)

    def update(s, smax, vt, c0):
        m_old = m_ref[:, c0:]
        m_new = jnp.maximum(m_old, smax)
        alpha = jnp.exp2(m_old - m_new)
        p = jnp.exp2(s - m_new).astype(_BF)
        if s.shape[0] < vt.shape[1]:
            p = jnp.concatenate(
                [p, jnp.zeros((vt.shape[1] - s.shape[0], s.shape[1]), _BF)], axis=0)
        vt1 = jnp.concatenate([vt, ones_row(vt.shape[1])], axis=0)
        acc_ref[:, c0:] = alpha * acc_ref[:, c0:] + _dot(vt1, p)
        m_ref[:, c0:] = m_new

    causal = (lax.broadcasted_iota(jnp.int32, (tk, tq), 0)
              <= lax.broadcasted_iota(jnp.int32, (tk, tq), 1))
    buf_a, buf_b = (sa_ref, ma_ref), (sb_ref, mb_ref)

    for qi in range(seq // tq):
        rows = slice(qi * tq, (qi + 1) * tq)
        nfull = qi * tq // tk
        nblk = nfull + tq // tk
        m_ref[...] = jnp.full((1, tq), NEG_INF, _F32)
        acc_ref[...] = jnp.zeros((HEAD_V + ONES_ROWS, tq), _F32)

        def diagonal(blk):
            return isinstance(blk, int) and blk >= nfull

        def col0(blk):
            return (blk - nfull) * tk if diagonal(blk) else 0

        def keys(blk):
            if isinstance(blk, int):
                return slice(blk * tk, (blk + 1) * tk)
            return pl.ds(pl.multiple_of(blk * tk, tk), tk)

        def produce(buf, blk):
            s_ref, smax_ref = buf
            c0 = col0(blk)
            s = _dot_nt(k_ref[0, keys(blk), :], q_ref[0, qi * tq + c0:(qi + 1) * tq, :])
            if diagonal(blk):
                s = jnp.where(causal[:, :tq - c0], s, NEG_INF)
            s_ref[:, c0:] = s
            smax_ref[:, c0:] = jnp.max(s, axis=0, keepdims=True)

        def consume(buf, blk):
            s_ref, smax_ref = buf
            c0 = col0(blk)
            update(s_ref[:, c0:], smax_ref[:, c0:], vt_ref[0, :, keys(blk)], c0)

        produce(buf_a, 0)
        s_meta = _dot_nt(km_ref[:N_META, :], q_ref[0, rows, :])
        update(s_meta, jnp.max(s_meta, axis=0, keepdims=True), vmt_ref[...], 0)

        def pair(i, carry):
            produce(buf_b, 2 * i + 1)
            consume(buf_a, 2 * i)
            produce(buf_a, 2 * i + 2)
            consume(buf_b, 2 * i + 1)
            return carry

        trips = (nfull - 1) // 2 if nfull >= 3 else 0
        if trips:
            lax.fori_loop(0, trips, pair, 0)
        done = 2 * trips
        for blk in range(done, nblk - 1):
            produce(buf_b if blk % 2 == 0 else buf_a, blk + 1)
            consume(buf_a if blk % 2 == 0 else buf_b, blk)
        consume(buf_a if (nblk - 1) % 2 == 0 else buf_b, nblk - 1)

        y = acc_ref[:HEAD_V, :] * (1.0 / acc_ref[HEAD_V:HEAD_V + 1, :])
        o_ref[0, rows, :] = (y.T * zs_ref[0, rows, :].astype(_F32)).astype(_BF)


def _attention(q, k, vt, k_meta, vt_meta, proj3, *, tq, tk):
    batch, seq = q.shape[0], q.shape[2]
    return pl.pallas_call(
        functools.partial(_attn_kernel, seq=seq, tq=tq, tk=tk),
        grid=(batch, HEADS),
        in_specs=[
            pl.BlockSpec((None, 1, seq, QK_PAD), lambda b, h: (b, h, 0, 0)),
            pl.BlockSpec((None, 1, seq, QK_PAD), lambda b, h: (b, h, 0, 0)),
            pl.BlockSpec((1, HEAD_V, seq), lambda b, h: (b, h, 0)),
            pl.BlockSpec((None, CHUNK, QK_PAD), lambda b, h: (h, 0, 0)),
            pl.BlockSpec((HEAD_V, CHUNK), lambda b, h: (h, 0)),
            pl.BlockSpec((1, seq, HEAD_V), lambda b, h: (b, 0, ZM_OFF // HEAD_V + h)),
        ],
        out_specs=pl.BlockSpec((1, seq, HEAD_V), lambda b, h: (b, 0, h)),
        out_shape=jax.ShapeDtypeStruct((batch, seq, HEADS * HEAD_V), _BF),
        scratch_shapes=[pltpu.VMEM((1, tq), _F32),
                        pltpu.VMEM((HEAD_V + ONES_ROWS, tq), _F32),
                        pltpu.VMEM((tk, tq), _F32), pltpu.VMEM((tk, tq), _F32),
                        pltpu.VMEM((1, tq), _F32), pltpu.VMEM((1, tq), _F32)],
        compiler_params=pltpu.CompilerParams(
            dimension_semantics=("parallel", "parallel"), vmem_limit_bytes=VMEM_LIMIT),
        name="attention",
    )(q, k, vt, k_meta, vt_meta, proj3)


def _rope128(t, c, s):
    return t * c + pltpu.roll(t, 64, 1) * s


def _ret_kernel(rq_ref, rk_ref, rv_ref, zs_ref, rkm_ref, rvm_ref, c_ref, s_ref, cm_ref, sm_ref,
                dec_ref, zeta_ref, xi_ref, gch_ref, gw_ref, gb_ref, o_ref, *, seq, group):
    kscale = RET_D ** -0.5
    dec = dec_ref[0]
    zeta = zeta_ref[0]
    xi = xi_ref[0]
    gch = gch_ref[0]
    gw = gw_ref[...]
    gb = gb_ref[...]

    km = _rope128(rkm_ref[...].astype(_F32), cm_ref[...], sm_ref[...]) * kscale
    state0 = _dot_tn((km * pltpu.roll(zeta, N_META, 0)).astype(_BF), rvm_ref[...])

    rows = group * CHUNK
    bdims = ((0,), (0,))

    def body(gi, state):
        rs = pl.multiple_of(gi * rows, rows)
        c = c_ref[pl.ds(rs, rows), :]
        s = s_ref[pl.ds(rs, rows), :]
        q = _rope128(rq_ref[0, pl.ds(rs, rows), :].astype(_F32), c, s)
        k = _rope128(rk_ref[0, pl.ds(rs, rows), :].astype(_F32), c, s) * kscale
        q3 = q.reshape(group, CHUNK, RET_D)
        k3 = k.reshape(group, CHUNK, RET_D)
        v3 = rv_ref[0, pl.ds(rs, rows), :].reshape(group, CHUNK, RET_D)
        kz = jnp.swapaxes((k3 * zeta[None]).astype(_BF), 1, 2)
        kv = lax.dot_general(kz, v3, (((2,), (1,)), bdims), preferred_element_type=_F32)
        qb = q3.astype(_BF)
        sc = lax.dot_general(qb, k3.astype(_BF), (((2,), (2,)), bdims),
                             preferred_element_type=_F32) * dec[None]
        states = []
        for ci in range(group):
            states.append(state.astype(_BF))
            state = state * gch + kv[ci]
        o = (lax.dot_general(sc.astype(_BF), v3, (((2,), (1,)), bdims),
                             preferred_element_type=_F32)
             + lax.dot_general(qb, jnp.stack(states), (((2,), (1,)), bdims),
                               preferred_element_type=_F32) * xi[None])
        o = o.reshape(rows, RET_D)
        mu = jnp.mean(o, axis=-1, keepdims=True)
        d = o - mu
        var = jnp.mean(d * d, axis=-1, keepdims=True)
        y = d * lax.rsqrt(var + GN_EPS) * gw + gb
        o_ref[0, pl.ds(rs, rows), :] = (
            y * zs_ref[0, pl.ds(rs, rows), :].astype(_F32)).astype(_BF)
        return state

    lax.fori_loop(0, seq // rows, body, state0)


def _retention(proj3, proj_meta, tabs_x, tabs_m, consts, gn_w, gn_b, *, group):
    batch, seq = proj3.shape[0], proj3.shape[1]
    col = lambda off: (lambda b, h: (b, 0, off // RET_D + h))
    mcol = lambda off: (lambda b, h: (0, off // RET_D + h))
    head3 = pl.BlockSpec((1, CHUNK, RET_D), lambda b, h: (h, 0, 0))
    return pl.pallas_call(
        functools.partial(_ret_kernel, seq=seq, group=group),
        grid=(batch, HEADS),
        in_specs=[
            pl.BlockSpec((1, seq, RET_D), col(RQ_OFF)),
            pl.BlockSpec((1, seq, RET_D), col(RK_OFF)),
            pl.BlockSpec((1, seq, RET_D), col(RV_OFF)),
            pl.BlockSpec((1, seq, RET_D), col(ZR_OFF)),
            pl.BlockSpec((CHUNK, RET_D), mcol(RK_OFF)),
            pl.BlockSpec((CHUNK, RET_D), mcol(RV_OFF)),
            pl.BlockSpec((seq, RET_D), lambda b, h: (0, 0)),
            pl.BlockSpec((seq, RET_D), lambda b, h: (0, 0)),
            pl.BlockSpec((CHUNK, RET_D), lambda b, h: (0, 0)),
            pl.BlockSpec((CHUNK, RET_D), lambda b, h: (0, 0)),
            head3, head3, head3, head3,
            pl.BlockSpec((1, RET_D), lambda b, h: (0, h)),
            pl.BlockSpec((1, RET_D), lambda b, h: (0, h)),
        ],
        out_specs=pl.BlockSpec((1, seq, RET_D), lambda b, h: (b, 0, h)),
        out_shape=jax.ShapeDtypeStruct((batch, seq, HEADS * RET_D), _BF),
        compiler_params=pltpu.CompilerParams(
            dimension_semantics=("parallel", "parallel"), vmem_limit_bytes=VMEM_LIMIT),
        name="retention",
    )(proj3, proj3, proj3, proj3, proj_meta, proj_meta, *tabs_x, *tabs_m, *consts, gn_w, gn_b)


def _out_proj_kernel(ym_ref, yr_ref, g0_ref, g1_ref, x_ref, wbm_ref, wbr_ref, wo_ref, fw_ref, o_ref):
    m1 = _dot(ym_ref[...], wbm_ref[...])
    m2 = _dot(yr_ref[...], wbr_ref[...])
    merged = g0_ref[...].astype(_F32) * m1 + g1_ref[...].astype(_F32) * m2
    h = x_ref[...] + _dot(merged.astype(_BF), wo_ref[...])
    ms = jnp.mean(h * h, axis=-1, keepdims=True)
    o_ref[...] = h * lax.rsqrt(ms + NORM_EPS) * fw_ref[...]


def _out_proj(y_mla, y_ret, proj, x2d, wbm, wbr, wo, fw, *, tm):
    rows = x2d.shape[0]
    width = HEADS * HEAD_V
    const = lambda shape: pl.BlockSpec(shape, lambda i: (0, 0), pipeline_mode=pl.Buffered(1))
    return pl.pallas_call(
        _out_proj_kernel,
        grid=(rows // tm,),
        in_specs=[
            pl.BlockSpec((tm, width), lambda i: (i, 0)),
            pl.BlockSpec((tm, width), lambda i: (i, 0)),
            pl.BlockSpec((tm, D_MODEL), lambda i: (i, GATE_OFF // D_MODEL)),
            pl.BlockSpec((tm, D_MODEL), lambda i: (i, GATE_OFF // D_MODEL + 1)),
            pl.BlockSpec((tm, D_MODEL), lambda i: (i, 0)),
            const((width, D_MODEL)), const((width, D_MODEL)), const((D_MODEL, D_MODEL)),
            const((1, D_MODEL)),
        ],
        out_specs=pl.BlockSpec((tm, D_MODEL), lambda i: (i, 0)),
        out_shape=jax.ShapeDtypeStruct((rows, D_MODEL), _F32),
        compiler_params=pltpu.CompilerParams(
            dimension_semantics=("parallel",), vmem_limit_bytes=VMEM_LIMIT),
        name="out_proj",
    )(y_mla, y_ret, proj, proj, x2d, wbm, wbr, wo, fw)


def _rope_tables(seq):
    pos = np.arange(CHUNK + seq, dtype=np.float64) - META_PAD
    z32 = np.zeros((pos.shape[0], 32))
    inv64 = ROPE_BASE ** (-np.arange(0, ROPE, 2, dtype=np.float64) / ROPE)
    ang = pos[:, None] * inv64[None, :]
    cos, sin = np.cos(ang), np.sin(ang)
    c64 = np.concatenate([cos, cos, z32, z32], axis=1)
    s1 = np.concatenate([-sin, z32, z32, z32], axis=1)
    s2 = np.concatenate([z32, sin, z32, z32], axis=1)
    inv128 = ROPE_BASE ** (-np.arange(0, RET_D, 2, dtype=np.float64) / RET_D)
    ang = pos[:, None] * inv128[None, :]
    cos, sin = np.cos(ang), np.sin(ang)
    c128 = np.concatenate([cos, cos], axis=1)
    s128 = np.concatenate([-sin, sin], axis=1)
    return tuple(t.astype(np.float32) for t in (c64, s1, s2, c128, s128))


def _decay_tables():
    log_g = jnp.log1p(-(2.0 ** (-5.0 - jnp.arange(HEADS, dtype=_F32))))
    n = jnp.arange(CHUNK, dtype=_F32)
    diff = n[:, None] - n[None, :]
    dec = jnp.where(diff >= 0, jnp.exp(log_g[:, None, None] * jnp.maximum(diff, 0.0)), 0.0)
    ones = jnp.ones((1, 1, RET_D), _F32)
    zeta = jnp.exp(log_g[:, None] * (CHUNK - 1.0 - n))[:, :, None] * ones
    xi = jnp.exp(log_g[:, None] * (n + 1.0))[:, :, None] * ones
    gch = jnp.exp(log_g * CHUNK)[:, None, None] * jnp.ones((1, CHUNK, RET_D), _F32)
    return dec, zeta, xi, gch


def _pack_w_in_kernel(w_ref, o_ref, tail_ref, *, tr):
    j = pl.program_id(0)
    pad = ZM_OFF - KPE_END

    @pl.when(j == 0)
    def _():
        o_ref[:KPE_END, :] = w_ref[:KPE_END, :].astype(_BF)
        o_ref[KPE_END:, :] = jnp.zeros((tr - KPE_END, D_MODEL), _BF)

    @pl.when(j > 0)
    def _():
        o_ref[:pad, :] = tail_ref[...]
        o_ref[pad:, :] = w_ref[:tr - pad, :].astype(_BF)

    @pl.when(j + 1 < pl.num_programs(0))
    def _():
        tail_ref[...] = w_ref[tr - pad:, :].astype(_BF)


def _pack_w_in(w_in_t, *, tr):
    assert tr == ZM_OFF and IN_PAD % tr == 0
    steps = IN_PAD // tr
    width = w_in_t.shape[1]
    assert (steps - 1) * tr + tr - (ZM_OFF - KPE_END) == width
    return pl.pallas_call(
        functools.partial(_pack_w_in_kernel, tr=tr),
        grid=(steps,),
        in_specs=[pl.BlockSpec((None, tr, D_MODEL), lambda j: (0, j, 0))],
        out_specs=pl.BlockSpec((tr, D_MODEL), lambda j: (j, 0)),
        out_shape=jax.ShapeDtypeStruct((IN_PAD, D_MODEL), _BF),
        scratch_shapes=[pltpu.VMEM((ZM_OFF - KPE_END, D_MODEL), _BF)],
        compiler_params=pltpu.CompilerParams(
            dimension_semantics=("arbitrary",), vmem_limit_bytes=VMEM_LIMIT),
        name="pack_w_in",
    )(w_in_t)


def _pack_w_uq(w_uq):
    w = w_uq.reshape(Q_RANK, HEADS, NOPE + ROPE)
    w = jnp.pad(w, ((0, 0), (0, 0), (0, QK_PAD - NOPE - ROPE)))
    return w.reshape(Q_RANK, HEADS * QK_PAD).astype(_BF)


def _split_w_ukv(w_ukv):
    w = w_ukv.reshape(KV_RANK, HEADS, NOPE + HEAD_V)
    wk = w[:, :, :NOPE].reshape(KV_RANK, HEADS * NOPE)
    wvt = w[:, :, NOPE:].reshape(KV_RANK, HEADS * HEAD_V).T
    return wk.astype(_BF), wvt.astype(_BF)


def kernel(x, meta, norm_w, w_in, mla_q_norm_w, mla_w_uq, mla_kv_norm_w, mla_w_ukv, ret_gn_w,
           ret_gn_b, w_branch_mla, w_branch_ret, w_out, final_norm_w):
    batch, seq, _ = x.shape
    rows = batch * seq
    x2d = x.reshape(rows, D_MODEL)
    meta_chunk = jnp.pad(meta.astype(x.dtype), ((0, META_PAD), (0, 0)))

    w_in_bf = _pack_w_in(jnp.swapaxes(w_in, 1, 2), tr=ZM_OFF)
    wq = _pack_w_uq(mla_w_uq[0])
    wk, wvt = _split_w_ukv(mla_w_ukv[0])
    qnw, kvnw = mla_q_norm_w, mla_kv_norm_w
    scale = float((NOPE + ROPE) ** -0.5 * np.log2(np.e))

    c64, s1, s2, c128, s128 = _rope_tables(seq)
    meta_rows = slice(META_PAD, META_PAD + CHUNK)
    tabs64_x = (c64[CHUNK:], s1[CHUNK:], s2[CHUNK:])
    tabs64_m = (c64[meta_rows], s1[meta_rows], s2[meta_rows])
    tabs128_x = (c128[CHUNK:], s128[CHUNK:])
    tabs128_m = (c128[meta_rows], s128[meta_rows])

    proj = _in_proj(x2d, norm_w, w_in_bf, IN_PAD, tm=IN_PROJ_TM, tn=IN_PROJ_TN)
    proj_m = _in_proj(meta_chunk, norm_w, w_in_bf, GATE_OFF, tm=CHUNK, tn=IN_PROJ_TN)

    q, k, vt = _mla_prep(proj, qnw, kvnw, wq, wk, wvt, tabs64_x, tm=MLA_PREP_TM, scale=scale)
    _, k_m, vt_m = _mla_prep(proj_m, qnw, kvnw, wq, wk, wvt, tabs64_m, tm=CHUNK, scale=scale)

    proj3 = proj.reshape(batch, seq, IN_PAD)
    y_mla = _attention(q, k, vt, k_m[0], vt_m[0], proj3, tq=ATTN_TQ, tk=ATTN_TK)
    y_ret = _retention(proj3, proj_m, tabs128_x, tabs128_m, _decay_tables(),
                       ret_gn_w, ret_gn_b, group=RET_GROUP)

    out = _out_proj(y_mla.reshape(rows, -1), y_ret.reshape(rows, -1), proj, x2d,
                    w_branch_mla[0].astype(_BF), w_branch_ret[0].astype(_BF),
                    w_out[0].astype(_BF), final_norm_w.reshape(1, D_MODEL), tm=OUT_PROJ_TM)
    return out.reshape(batch, seq, D_MODEL)
```

```python
import functools

import numpy as np
import jax
import jax.numpy as jnp
from jax import lax
from jax.experimental import pallas as pl
from jax.experimental.pallas import tpu as pltpu

D_MODEL = 2048
N_META = 16
CHUNK = 128
META_PAD = CHUNK - N_META
HEADS = 8
NOPE = 128
ROPE = 64
HEAD_V = 128
Q_RANK = 512
KV_RANK = 256
QK_PAD = 256
ONES_ROWS = 16
RET_D = 128
ROPE_BASE = 10000.0
NORM_EPS = 1e-6
GN_EPS = 1e-5
NEG_INF = -1e30

CQ_OFF = 0
CKV_OFF = 512
KPE_OFF = 768
ZM_OFF = 1024
RQ_OFF = 2048
RK_OFF = 3072
RV_OFF = 4096
ZR_OFF = 5120
GATE_OFF = 6144
IN_PAD = 10240
KPE_END = KPE_OFF + ROPE

V7X_VMEM_BYTES = 64 * 1024 * 1024
VMEM_LIMIT = V7X_VMEM_BYTES - 8 * 1024 * 1024

XNORM_TM = 1024
IN_PROJ_TM = 2048
IN_PROJ_TN = 1024
MLA_PREP_TM = 1024
ATTN_TQ = 1024
ATTN_TK = 512
RET_GROUP = 32
OUT_PROJ_TM = 512

_BF = jnp.bfloat16
_F32 = jnp.float32


def _dot(a, b):
    return jnp.dot(a, b, preferred_element_type=_F32)


def _dot_nt(a, b):
    return lax.dot_general(a, b, (((1,), (1,)), ((), ())), preferred_element_type=_F32)


def _dot_tn(a, b):
    return lax.dot_general(a, b, (((0,), (0,)), ((), ())), preferred_element_type=_F32)


def _sigmoid(x):
    return 0.5 * jnp.tanh(0.5 * x) + 0.5


def _xnorm_kernel(x_ref, nw_ref, o_ref):
    x = x_ref[...]
    ms = jnp.mean(x * x, axis=-1, keepdims=True)
    o_ref[...] = (x * lax.rsqrt(ms + NORM_EPS) * nw_ref[...]).astype(_BF)


def _xnorm(x2d, norm_w, *, tm):
    rows = x2d.shape[0]
    return pl.pallas_call(
        _xnorm_kernel,
        grid=(rows // tm,),
        in_specs=[pl.BlockSpec((tm, D_MODEL), lambda i: (i, 0)),
                  pl.BlockSpec((1, D_MODEL), lambda i: (0, 0))],
        out_specs=pl.BlockSpec((tm, D_MODEL), lambda i: (i, 0)),
        out_shape=jax.ShapeDtypeStruct((rows, D_MODEL), _BF),
        compiler_params=pltpu.CompilerParams(
            dimension_semantics=("parallel",), vmem_limit_bytes=VMEM_LIMIT),
        name="xnorm",
    )(x2d, norm_w)


def _in_proj_kernel(xn_ref, w_ref, o_ref, *, tn):
    j = pl.program_id(1)
    acc = _dot_nt(xn_ref[...], w_ref[...])
    col = j * tn
    is_gate = col >= GATE_OFF
    is_z = ((col >= ZM_OFF) & (col < RQ_OFF)) | ((col >= ZR_OFF) & (col < GATE_OFF))

    @pl.when(is_gate)
    def _():
        o_ref[...] = _sigmoid(acc).astype(_BF)

    @pl.when(is_z)
    def _():
        o_ref[...] = (acc * _sigmoid(acc)).astype(_BF)

    @pl.when(jnp.logical_not(is_gate | is_z))
    def _():
        o_ref[...] = acc.astype(_BF)


def _in_proj(xn2d, w_in_bf, cols, *, tm, tn):
    rows = xn2d.shape[0]
    assert all(off % tn == 0 for off in (ZM_OFF, RQ_OFF, ZR_OFF, GATE_OFF, IN_PAD, cols))
    return pl.pallas_call(
        functools.partial(_in_proj_kernel, tn=tn),
        grid=(rows // tm, cols // tn),
        in_specs=[
            pl.BlockSpec((tm, D_MODEL), lambda i, j: (i, 0)),
            pl.BlockSpec((tn, D_MODEL), lambda i, j: (j, 0)),
        ],
        out_specs=pl.BlockSpec((tm, tn), lambda i, j: (i, j)),
        out_shape=jax.ShapeDtypeStruct((rows, cols), _BF),
        compiler_params=pltpu.CompilerParams(
            dimension_semantics=("parallel", "arbitrary"), vmem_limit_bytes=VMEM_LIMIT),
        name="in_proj",
    )(xn2d, w_in_bf)


def _rms(x, w):
    ms = jnp.mean(x * x, axis=-1, keepdims=True)
    return x * lax.rsqrt(ms + NORM_EPS) * w


def _rope64(t, c, s1, s2):
    return t * c + pltpu.roll(t, 96, 1) * s1 + pltpu.roll(t, 32, 1) * s2


def _mla_prep_kernel(cq_ref, ckv_ref, kpe_ref, qnw_ref, kvnw_ref, wq_ref, wk_ref, wvt_ref,
                     c_ref, s1_ref, s2_ref, q_ref, k_ref, vt_ref, *, scale):
    c, s1, s2 = c_ref[...], s1_ref[...], s2_ref[...]

    cqn = _rms(cq_ref[...].astype(_F32), qnw_ref[...]).astype(_BF)
    q = _dot(cqn, wq_ref[...])
    for h in range(HEADS):
        lo = h * QK_PAD
        q_ref[0, h, :, :NOPE] = (q[:, lo:lo + NOPE] * scale).astype(_BF)
        q_ref[0, h, :, NOPE:] = (
            _rope64(q[:, lo + NOPE:lo + QK_PAD], c, s1, s2) * scale).astype(_BF)

    ckvn = _rms(ckv_ref[...].astype(_F32), kvnw_ref[...]).astype(_BF)
    kn = _dot(ckvn, wk_ref[...])
    kpe = _rope64(kpe_ref[...].astype(_F32), c, s1, s2).astype(_BF)
    for h in range(HEADS):
        k_ref[0, h, :, :NOPE] = kn[:, h * NOPE:(h + 1) * NOPE].astype(_BF)
        k_ref[0, h, :, NOPE:] = kpe
    vt_ref[0] = _dot_nt(wvt_ref[...], ckvn).astype(_BF)


def _mla_prep(proj, qnw, kvnw, wq, wk, wvt, tabs, *, tm, scale):
    rows = proj.shape[0]
    seq = tabs[0].shape[0]
    tab_blocks = seq // tm
    full = lambda shape: pl.BlockSpec(shape, lambda i: (0, 0))
    tab_spec = pl.BlockSpec((tm, 128), lambda i: (i % tab_blocks, 0))
    return pl.pallas_call(
        functools.partial(_mla_prep_kernel, scale=scale),
        grid=(rows // tm,),
        in_specs=[
            pl.BlockSpec((tm, Q_RANK), lambda i: (i, CQ_OFF // Q_RANK)),
            pl.BlockSpec((tm, KV_RANK), lambda i: (i, CKV_OFF // KV_RANK)),
            pl.BlockSpec((tm, 128), lambda i: (i, KPE_OFF // 128)),
            full((1, Q_RANK)), full((1, KV_RANK)),
            full((Q_RANK, HEADS * QK_PAD)), full((KV_RANK, HEADS * NOPE)),
            full((HEADS * HEAD_V, KV_RANK)),
            tab_spec, tab_spec, tab_spec,
        ],
        out_specs=[
            pl.BlockSpec((1, HEADS, tm, QK_PAD), lambda i: (i // tab_blocks, 0, i % tab_blocks, 0)),
            pl.BlockSpec((1, HEADS, tm, QK_PAD), lambda i: (i // tab_blocks, 0, i % tab_blocks, 0)),
            pl.BlockSpec((1, HEADS * HEAD_V, tm), lambda i: (i // tab_blocks, 0, i % tab_blocks)),
        ],
        out_shape=[
            jax.ShapeDtypeStruct((rows // seq, HEADS, seq, QK_PAD), _BF),
            jax.ShapeDtypeStruct((rows // seq, HEADS, seq, QK_PAD), _BF),
            jax.ShapeDtypeStruct((rows // seq, HEADS * HEAD_V, seq), _BF),
        ],
        compiler_params=pltpu.CompilerParams(
            dimension_semantics=("parallel",), vmem_limit_bytes=VMEM_LIMIT),
        name="mla_prep",
    )(proj, proj, proj, qnw, kvnw, wq, wk, wvt, *tabs)


def _attn_kernel(q_ref, k_ref, vt_ref, km_ref, vmt_ref, zs_ref, o_ref, m_ref, acc_ref,
                 sa_ref, sb_ref, ma_ref, mb_ref, *, seq, tq, tk):
    def ones_row(width):
        first = lax.broadcasted_iota(jnp.int32, (ONES_ROWS, width), 0) == 0
        return jnp.where(first, 1.0, 0.0).astype(_BF)

    def update(s, smax, vt, c0):
        m_old = m_ref[:, c0:]
        m_new = jnp.maximum(m_old, smax)
        alpha = jnp.exp2(m_old - m_new)
        p = jnp.exp2(s - m_new).astype(_BF)
        if s.shape[0] < vt.shape[1]:
            p = jnp.concatenate(
                [p, jnp.zeros((vt.shape[1] - s.shape[0], s.shape[1]), _BF)], axis=0)
        vt1 = jnp.concatenate([vt, ones_row(vt.shape[1])], axis=0)
        acc_ref[:, c0:] = alpha * acc_ref[:, c0:] + _dot(vt1, p)
        m_ref[:, c0:] = m_new

    causal = (lax.broadcasted_iota(jnp.int32, (tk, tq), 0)
              <= lax.broadcasted_iota(jnp.int32, (tk, tq), 1))
    buf_a, buf_b = (sa_ref, ma_ref), (sb_ref, mb_ref)

    for qi in range(seq // tq):
        rows = slice(qi * tq, (qi + 1) * tq)
        nfull = qi * tq // tk
        nblk = nfull + tq // tk
        m_ref[...] = jnp.full((1, tq), NEG_INF, _F32)
        acc_ref[...] = jnp.zeros((HEAD_V + ONES_ROWS, tq), _F32)

        def diagonal(blk):
            return isinstance(blk, int) and blk >= nfull

        def col0(blk):
            return (blk - nfull) * tk if diagonal(blk) else 0

        def keys(blk):
            if isinstance(blk, int):
                return slice(blk * tk, (blk + 1) * tk)
            return pl.ds(pl.multiple_of(blk * tk, tk), tk)

        def produce(buf, blk):
            s_ref, smax_ref = buf
            c0 = col0(blk)
            s = _dot_nt(k_ref[0, keys(blk), :], q_ref[0, qi * tq + c0:(qi + 1) * tq, :])
            if diagonal(blk):
                s = jnp.where(causal[:, :tq - c0], s, NEG_INF)
            s_ref[:, c0:] = s
            smax_ref[:, c0:] = jnp.max(s, axis=0, keepdims=True)

        def consume(buf, blk):
            s_ref, smax_ref = buf
            c0 = col0(blk)
            update(s_ref[:, c0:], smax_ref[:, c0:], vt_ref[0, :, keys(blk)], c0)

        produce(buf_a, 0)
        s_meta = _dot_nt(km_ref[:N_META, :], q_ref[0, rows, :])
        update(s_meta, jnp.max(s_meta, axis=0, keepdims=True), vmt_ref[...], 0)

        def pair(i, carry):
            produce(buf_b, 2 * i + 1)
            consume(buf_a, 2 * i)
            produce(buf_a, 2 * i + 2)
            consume(buf_b, 2 * i + 1)
            return carry

        trips = (nfull - 1) // 2 if nfull >= 3 else 0
        if trips:
            lax.fori_loop(0, trips, pair, 0)
        done = 2 * trips
        for blk in range(done, nblk - 1):
            produce(buf_b if blk % 2 == 0 else buf_a, blk + 1)
            consume(buf_a if blk % 2 == 0 else buf_b, blk)
        consume(buf_a if (nblk - 1) % 2 == 0 else buf_b, nblk - 1)

        y = acc_ref[:HEAD_V, :] * (1.0 / acc_ref[HEAD_V:HEAD_V + 1, :])
        o_ref[0, rows, :] = (y.T * zs_ref[0, rows, :].astype(_F32)).astype(_BF)


def _attention(q, k, vt, k_meta, vt_meta, proj3, *, tq, tk):
    batch, seq = q.shape[0], q.shape[2]
    return pl.pallas_call(
        functools.partial(_attn_kernel, seq=seq, tq=tq, tk=tk),
        grid=(batch, HEADS),
        in_specs=[
            pl.BlockSpec((None, 1, seq, QK_PAD), lambda b, h: (b, h, 0, 0)),
            pl.BlockSpec((None, 1, seq, QK_PAD), lambda b, h: (b, h, 0, 0)),
            pl.BlockSpec((1, HEAD_V, seq), lambda b, h: (b, h, 0)),
            pl.BlockSpec((None, CHUNK, QK_PAD), lambda b, h: (h, 0, 0)),
            pl.BlockSpec((HEAD_V, CHUNK), lambda b, h: (h, 0)),
            pl.BlockSpec((1, seq, HEAD_V), lambda b, h: (b, 0, ZM_OFF // HEAD_V + h)),
        ],
        out_specs=pl.BlockSpec((1, seq, HEAD_V), lambda b, h: (b, 0, h)),
        out_shape=jax.ShapeDtypeStruct((batch, seq, HEADS * HEAD_V), _BF),
        scratch_shapes=[pltpu.VMEM((1, tq), _F32),
                        pltpu.VMEM((HEAD_V + ONES_ROWS, tq), _F32),
                        pltpu.VMEM((tk, tq), _F32), pltpu.VMEM((tk, tq), _F32),
                        pltpu.VMEM((1, tq), _F32), pltpu.VMEM((1, tq), _F32)],
        compiler_params=pltpu.CompilerParams(
            dimension_semantics=("parallel", "parallel"), vmem_limit_bytes=VMEM_LIMIT),
        name="attention",
    )(q, k, vt, k_meta, vt_meta, proj3)


def _rope128(t, c, s):
    return t * c + pltpu.roll(t, 64, 1) * s


def _ret_kernel(rq_ref, rk_ref, rv_ref, zs_ref, rkm_ref, rvm_ref, c_ref, s_ref, cm_ref, sm_ref,
                dec_ref, zeta_ref, xi_ref, gch_ref, gw_ref, gb_ref, o_ref, *, seq, group):
    kscale = RET_D ** -0.5
    dec = dec_ref[0]
    zeta = zeta_ref[0]
    xi = xi_ref[0]
    gch = gch_ref[0]
    gw = gw_ref[...]
    gb = gb_ref[...]

    km = _rope128(rkm_ref[...].astype(_F32), cm_ref[...], sm_ref[...]) * kscale
    state0 = _dot_tn((km * pltpu.roll(zeta, N_META, 0)).astype(_BF), rvm_ref[...])

    rows = group * CHUNK
    bdims = ((0,), (0,))

    def body(gi, state):
        rs = pl.multiple_of(gi * rows, rows)
        c = c_ref[pl.ds(rs, rows), :]
        s = s_ref[pl.ds(rs, rows), :]
        q = _rope128(rq_ref[0, pl.ds(rs, rows), :].astype(_F32), c, s)
        k = _rope128(rk_ref[0, pl.ds(rs, rows), :].astype(_F32), c, s) * kscale
        q3 = q.reshape(group, CHUNK, RET_D)
        k3 = k.reshape(group, CHUNK, RET_D)
        v3 = rv_ref[0, pl.ds(rs, rows), :].reshape(group, CHUNK, RET_D)
        kz = jnp.swapaxes((k3 * zeta[None]).astype(_BF), 1, 2)
        kv = lax.dot_general(kz, v3, (((2,), (1,)), bdims), preferred_element_type=_F32)
        qb = q3.astype(_BF)
        sc = lax.dot_general(qb, k3.astype(_BF), (((2,), (2,)), bdims),
                             preferred_element_type=_F32) * dec[None]
        states = []
        for ci in range(group):
            states.append(state.astype(_BF))
            state = state * gch + kv[ci]
        o = (lax.dot_general(sc.astype(_BF), v3, (((2,), (1,)), bdims),
                             preferred_element_type=_F32)
             + lax.dot_general(qb, jnp.stack(states), (((2,), (1,)), bdims),
                               preferred_element_type=_F32) * xi[None])
        o = o.reshape(rows, RET_D)
        mu = jnp.mean(o, axis=-1, keepdims=True)
        d = o - mu
        var = jnp.mean(d * d, axis=-1, keepdims=True)
        y = d * lax.rsqrt(var + GN_EPS) * gw + gb
        o_ref[0, pl.ds(rs, rows), :] = (
            y * zs_ref[0, pl.ds(rs, rows), :].astype(_F32)).astype(_BF)
        return state

    lax.fori_loop(0, seq // rows, body, state0)


def _retention(proj3, proj_meta, tabs_x, tabs_m, consts, gn_w, gn_b, *, group):
    batch, seq = proj3.shape[0], proj3.shape[1]
    col = lambda off: (lambda b, h: (b, 0, off // RET_D + h))
    mcol = lambda off: (lambda b, h: (0, off // RET_D + h))
    head3 = pl.BlockSpec((1, CHUNK, RET_D), lambda b, h: (h, 0, 0))
    return pl.pallas_call(
        functools.partial(_ret_kernel, seq=seq, group=group),
        grid=(batch, HEADS),
        in_specs=[
            pl.BlockSpec((1, seq, RET_D), col(RQ_OFF)),
            pl.BlockSpec((1, seq, RET_D), col(RK_OFF)),
            pl.BlockSpec((1, seq, RET_D), col(RV_OFF)),
            pl.BlockSpec((1, seq, RET_D), col(ZR_OFF)),
            pl.BlockSpec((CHUNK, RET_D), mcol(RK_OFF)),
            pl.BlockSpec((CHUNK, RET_D), mcol(RV_OFF)),
            pl.BlockSpec((seq, RET_D), lambda b, h: (0, 0)),
            pl.BlockSpec((seq, RET_D), lambda b, h: (0, 0)),
            pl.BlockSpec((CHUNK, RET_D), lambda b, h: (0, 0)),
            pl.BlockSpec((CHUNK, RET_D), lambda b, h: (0, 0)),
            head3, head3, head3, head3,
            pl.BlockSpec((1, RET_D), lambda b, h: (0, h)),
            pl.BlockSpec((1, RET_D), lambda b, h: (0, h)),
        ],
        out_specs=pl.BlockSpec((1, seq, RET_D), lambda b, h: (b, 0, h)),
        out_shape=jax.ShapeDtypeStruct((batch, seq, HEADS * RET_D), _BF),
        compiler_params=pltpu.CompilerParams(
            dimension_semantics=("parallel", "parallel"), vmem_limit_bytes=VMEM_LIMIT),
        name="retention",
    )(proj3, proj3, proj3, proj3, proj_meta, proj_meta, *tabs_x, *tabs_m, *consts, gn_w, gn_b)


def _out_proj_kernel(ym_ref, yr_ref, g0_ref, g1_ref, x_ref, wbm_ref, wbr_ref, wo_ref, fw_ref, o_ref):
    m1 = _dot(ym_ref[...], wbm_ref[...])
    m2 = _dot(yr_ref[...], wbr_ref[...])
    merged = g0_ref[...].astype(_F32) * m1 + g1_ref[...].astype(_F32) * m2
    h = x_ref[...] + _dot(merged.astype(_BF), wo_ref[...])
    ms = jnp.mean(h * h, axis=-1, keepdims=True)
    o_ref[...] = h * lax.rsqrt(ms + NORM_EPS) * fw_ref[...]


def _out_proj(y_mla, y_ret, proj, x2d, wbm, wbr, wo, fw, *, tm):
    rows = x2d.shape[0]
    width = HEADS * HEAD_V
    const = lambda shape: pl.BlockSpec(shape, lambda i: (0, 0), pipeline_mode=pl.Buffered(1))
    return pl.pallas_call(
        _out_proj_kernel,
        grid=(rows // tm,),
        in_specs=[
            pl.BlockSpec((tm, width), lambda i: (i, 0)),
            pl.BlockSpec((tm, width), lambda i: (i, 0)),
            pl.BlockSpec((tm, D_MODEL), lambda i: (i, GATE_OFF // D_MODEL)),
            pl.BlockSpec((tm, D_MODEL), lambda i: (i, GATE_OFF // D_MODEL + 1)),
            pl.BlockSpec((tm, D_MODEL), lambda i: (i, 0)),
            const((width, D_MODEL)), const((width, D_MODEL)), const((D_MODEL, D_MODEL)),
            const((1, D_MODEL)),
        ],
        out_specs=pl.BlockSpec((tm, D_MODEL), lambda i: (i, 0)),
        out_shape=jax.ShapeDtypeStruct((rows, D_MODEL), _F32),
        compiler_params=pltpu.CompilerParams(
            dimension_semantics=("parallel",), vmem_limit_bytes=VMEM_LIMIT),
        name="out_proj",
    )(y_mla, y_ret, proj, proj, x2d, wbm, wbr, wo, fw)


def _rope_tables(seq):
    pos = np.arange(CHUNK + seq, dtype=np.float64) - META_PAD
    z32 = np.zeros((pos.shape[0], 32))
    inv64 = ROPE_BASE ** (-np.arange(0, ROPE, 2, dtype=np.float64) / ROPE)
    ang = pos[:, None] * inv64[None, :]
    cos, sin = np.cos(ang), np.sin(ang)
    c64 = np.concatenate([cos, cos, z32, z32], axis=1)
    s1 = np.concatenate([-sin, z32, z32, z32], axis=1)
    s2 = np.concatenate([z32, sin, z32, z32], axis=1)
    inv128 = ROPE_BASE ** (-np.arange(0, RET_D, 2, dtype=np.float64) / RET_D)
    ang = pos[:, None] * inv128[None, :]
    cos, sin = np.cos(ang), np.sin(ang)
    c128 = np.concatenate([cos, cos], axis=1)
    s128 = np.concatenate([-sin, sin], axis=1)
    return tuple(t.astype(np.float32) for t in (c64, s1, s2, c128, s128))


def _decay_tables():
    log_g = jnp.log1p(-(2.0 ** (-5.0 - jnp.arange(HEADS, dtype=_F32))))
    n = jnp.arange(CHUNK, dtype=_F32)
    diff = n[:, None] - n[None, :]
    dec = jnp.where(diff >= 0, jnp.exp(log_g[:, None, None] * jnp.maximum(diff, 0.0)), 0.0)
    ones = jnp.ones((1, 1, RET_D), _F32)
    zeta = jnp.exp(log_g[:, None] * (CHUNK - 1.0 - n))[:, :, None] * ones
    xi = jnp.exp(log_g[:, None] * (n + 1.0))[:, :, None] * ones
    gch = jnp.exp(log_g * CHUNK)[:, None, None] * jnp.ones((1, CHUNK, RET_D), _F32)
    return dec, zeta, xi, gch


def _pack_w_in_kernel(w_ref, o_ref, tail_ref, *, tr):
    j = pl.program_id(0)
    pad = ZM_OFF - KPE_END

    @pl.when(j == 0)
    def _():
        o_ref[:KPE_END, :] = w_ref[:KPE_END, :].astype(_BF)
        o_ref[KPE_END:, :] = jnp.zeros((tr - KPE_END, D_MODEL), _BF)

    @pl.when(j > 0)
    def _():
        o_ref[:pad, :] = tail_ref[...]
        o_ref[pad:, :] = w_ref[:tr - pad, :].astype(_BF)

    @pl.when(j + 1 < pl.num_programs(0))
    def _():
        tail_ref[...] = w_ref[tr - pad:, :].astype(_BF)


def _pack_w_in(w_in_t, *, tr):
    assert tr == ZM_OFF and IN_PAD % tr == 0
    steps = IN_PAD // tr
    width = w_in_t.shape[1]
    assert (steps - 1) * tr + tr - (ZM_OFF - KPE_END) == width
    return pl.pallas_call(
        functools.partial(_pack_w_in_kernel, tr=tr),
        grid=(steps,),
        in_specs=[pl.BlockSpec((None, tr, D_MODEL), lambda j: (0, j, 0))],
        out_specs=pl.BlockSpec((tr, D_MODEL), lambda j: (j, 0)),
        out_shape=jax.ShapeDtypeStruct((IN_PAD, D_MODEL), _BF),
        scratch_shapes=[pltpu.VMEM((ZM_OFF - KPE_END, D_MODEL), _BF)],
        compiler_params=pltpu.CompilerParams(
            dimension_semantics=("arbitrary",), vmem_limit_bytes=VMEM_LIMIT),
        name="pack_w_in",
    )(w_in_t)


def _pack_w_uq(w_uq):
    w = w_uq.reshape(Q_RANK, HEADS, NOPE + ROPE)
    w = jnp.pad(w, ((0, 0), (0, 0), (0, QK_PAD - NOPE - ROPE)))
    return w.reshape(Q_RANK, HEADS * QK_PAD).astype(_BF)


def _split_w_ukv(w_ukv):
    w = w_ukv.reshape(KV_RANK, HEADS, NOPE + HEAD_V)
    wk = w[:, :, :NOPE].reshape(KV_RANK, HEADS * NOPE)
    wvt = w[:, :, NOPE:].reshape(KV_RANK, HEADS * HEAD_V).T
    return wk.astype(_BF), wvt.astype(_BF)


def kernel(x, meta, norm_w, w_in, mla_q_norm_w, mla_w_uq, mla_kv_norm_w, mla_w_ukv, ret_gn_w,
           ret_gn_b, w_branch_mla, w_branch_ret, w_out, final_norm_w):
    batch, seq, _ = x.shape
    rows = batch * seq
    x2d = x.reshape(rows, D_MODEL)
    meta_chunk = jnp.pad(meta.astype(x.dtype), ((0, META_PAD), (0, 0)))

    w_in_bf = _pack_w_in(jnp.swapaxes(w_in, 1, 2), tr=ZM_OFF)
    wq = _pack_w_uq(mla_w_uq[0])
    wk, wvt = _split_w_ukv(mla_w_ukv[0])
    qnw, kvnw = mla_q_norm_w, mla_kv_norm_w
    scale = float((NOPE + ROPE) ** -0.5 * np.log2(np.e))

    c64, s1, s2, c128, s128 = _rope_tables(seq)
    meta_rows = slice(META_PAD, META_PAD + CHUNK)
    tabs64_x = (c64[CHUNK:], s1[CHUNK:], s2[CHUNK:])
    tabs64_m = (c64[meta_rows], s1[meta_rows], s2[meta_rows])
    tabs128_x = (c128[CHUNK:], s128[CHUNK:])
    tabs128_m = (c128[meta_rows], s128[meta_rows])

    proj = _in_proj(_xnorm(x2d, norm_w, tm=XNORM_TM), w_in_bf, IN_PAD,
                    tm=IN_PROJ_TM, tn=IN_PROJ_TN)
    proj_m = _in_proj(_xnorm(meta_chunk, norm_w, tm=CHUNK), w_in_bf, GATE_OFF,
                      tm=CHUNK, tn=IN_PROJ_TN)

    q, k, vt = _mla_prep(proj, qnw, kvnw, wq, wk, wvt, tabs64_x, tm=MLA_PREP_TM, scale=scale)
    _, k_m, vt_m = _mla_prep(proj_m, qnw, kvnw, wq, wk, wvt, tabs64_m, tm=CHUNK, scale=scale)

    proj3 = proj.reshape(batch, seq, IN_PAD)
    y_mla = _attention(q, k, vt, k_m[0], vt_m[0], proj3, tq=ATTN_TQ, tk=ATTN_TK)
    y_ret = _retention(proj3, proj_m, tabs128_x, tabs128_m, _decay_tables(),
                       ret_gn_w, ret_gn_b, group=RET_GROUP)

    out = _out_proj(y_mla.reshape(rows, -1), y_ret.reshape(rows, -1), proj, x2d,
                    w_branch_mla[0].astype(_BF), w_branch_ret[0].astype(_BF),
                    w_out[0].astype(_BF), final_norm_w.reshape(1, D_MODEL), tm=OUT_PROJ_TM)
    return out.reshape(batch, seq, D_MODEL)
```
